```python
import math
import jax, jax.numpy as jnp
from jax import lax
import numpy as np

D_MODEL = 2048
BATCH = 8
SEQ = 4096
DEPTH = 4
DEC_BATCH = 8
DEC_SEQ = 32
PAST_LEN = 1024

CHUNK = 64
D_FF = 5632
C_CONV = 1024
CONV_WIDTH = 31
MLA_HEADS = 8
QK_NOPE = 128
QK_ROPE = 64
V_HEAD = 128
Q_LORA = 768
KV_LORA = 512
ROPE_THETA = 10000.0
MLA_SCALE = (QK_NOPE + QK_ROPE) ** -0.5
C_HEADS = 8
C_HEAD_DIM = 128
C_SCALE = C_HEAD_DIM ** -0.5
LEFT_CHUNKS = 8
LEFT_FRAMES = LEFT_CHUNKS * CHUNK
BAND = (LEFT_CHUNKS + 1) * CHUNK
REL_MAX = 128
N_REL = 2 * REL_MAX + 1
PLE_DIM = 256
N_BRANCH = 3
Q_BLOCK = 128
DN_ALPHA = (2 * DEPTH) ** 0.25
DN_BETA = (8 * DEPTH) ** -0.25
LN_EPS = 1e-5
RMS_EPS = 1e-6
NEG_INF = -1e30
IN_CONV = 2 * C_CONV
IN_Q = Q_LORA
IN_KV = KV_LORA + QK_ROPE
IN_C = 3 * C_HEADS * C_HEAD_DIM
IN_GATE = N_BRANCH * D_MODEL
D_IN = IN_CONV + IN_Q + IN_KV + IN_C + IN_GATE
IN_SPLITS = [IN_CONV, IN_CONV + IN_Q, IN_CONV + IN_Q + IN_KV, IN_CONV + IN_Q + IN_KV + IN_C]

kernel_name = "hybrid_streaming_conformer_mla_step"


def layer_norm(x, g, b):
    xf = x.astype(jnp.float32)
    mu = jnp.mean(xf, -1, keepdims=True)
    var = jnp.mean(jnp.square(xf - mu), -1, keepdims=True)
    return ((xf - mu) * lax.rsqrt(var + LN_EPS) * g + b).astype(x.dtype)


def rms_norm(x, g):
    xf = x.astype(jnp.float32)
    return (xf * lax.rsqrt(jnp.mean(xf * xf, -1, keepdims=True) + RMS_EPS) * g).astype(x.dtype)


def rope(x, pos):
    half = QK_ROPE // 2
    inv = ROPE_THETA ** (-jnp.arange(half, dtype=jnp.float32) / half)
    ang = pos.astype(jnp.float32)[:, None] * inv[None, :]
    ang = ang.reshape((ang.shape[0],) + (1,) * (x.ndim - 3) + (half,))
    cos, sin = jnp.cos(ang), jnp.sin(ang)
    x1 = x[..., :half].astype(jnp.float32)
    x2 = x[..., half:].astype(jnp.float32)
    return jnp.concatenate([x1 * cos - x2 * sin, x2 * cos + x1 * sin], -1).astype(x.dtype)


def swiglu(x, w_gate, w_up, w_down):
    return (jax.nn.silu(x @ w_gate) * (x @ w_up)) @ w_down


def conv_branch(u_glu, conv_hist, w):
    a, b = jnp.split(u_glu, 2, axis=-1)
    u = a * jax.nn.sigmoid(b)
    xin = jnp.concatenate([conv_hist.astype(u.dtype), u], axis=1)
    y = lax.conv_general_dilated(xin, w["conv_dw_w"][:, None, :], (1,), "VALID",
                                 dimension_numbers=("NWC", "WIO", "NWC"),
                                 feature_group_count=C_CONV) + w["conv_dw_b"]
    y = jax.nn.silu(layer_norm(y, w["conv_ln_g"], w["conv_ln_b"]))
    return y @ w["conv_pw"], xin[:, -(CONV_WIDTH - 1):]


def mla_project(cq, ckv_raw, pos, w):
    B, S = cq.shape[:2]
    q = (rms_norm(cq, w["mla_q_norm"]) @ w["q_up"]).reshape(B, S, MLA_HEADS, QK_NOPE + QK_ROPE)
    q_nope, q_rope = q[..., :QK_NOPE], rope(q[..., QK_NOPE:], pos)
    ckv = rms_norm(ckv_raw[..., :KV_LORA], w["mla_kv_norm"])
    k_rope = rope(ckv_raw[..., KV_LORA:], pos)
    return q_nope, q_rope, ckv, k_rope


def mla_core(q_nope, q_rope, q_pos, k_nope, k_rope, v, k_pos):
    s = (jnp.einsum("bqhd,bkhd->bhqk", q_nope, k_nope)
         + jnp.einsum("bqhr,bkr->bhqk", q_rope, k_rope)).astype(jnp.float32) * MLA_SCALE
    visible = (k_pos[None, :] // CHUNK) <= (q_pos[:, None] // CHUNK)
    p = jax.nn.softmax(jnp.where(visible, s, NEG_INF), axis=-1).astype(v.dtype)
    return jnp.einsum("bhqk,bkhd->bqhd", p, v)


def mla_blocked(q_nope, q_rope, q_pos, k_nope, k_rope, v, k_pos):
    B, S = q_nope.shape[:2]
    nb = S // Q_BLOCK

    def blocks(t):
        return jnp.moveaxis(t.reshape((B, nb, Q_BLOCK) + t.shape[2:]), 1, 0)

    out = lax.map(lambda a: mla_core(a[0], a[1], a[2], k_nope, k_rope, v, k_pos),
                  (blocks(q_nope), blocks(q_rope), q_pos.reshape(nb, Q_BLOCK)))
    return jnp.moveaxis(out, 0, 1).reshape(B, S, MLA_HEADS, V_HEAD)


def chunk_core(q, k, v, q_pos, k_pos, rel_bias):
    s = jnp.einsum("bqhd,bkhd->bhqk", q, k).astype(jnp.float32) * C_SCALE
    rel = jnp.clip(q_pos[:, None] - k_pos[None, :], -REL_MAX, REL_MAX) + REL_MAX
    s = s + rel_bias[:, rel].astype(jnp.float32)
    qc = q_pos[:, None] // CHUNK
    kc = k_pos[None, :] // CHUNK
    visible = (k_pos[None, :] >= 0) & (kc <= qc) & (kc >= qc - LEFT_CHUNKS)
    p = jax.nn.softmax(jnp.where(visible, s, NEG_INF), axis=-1).astype(v.dtype)
    return jnp.einsum("bhqk,bkhd->bqhd", p, v)


def chunk_band_prompt(q, k, v, rel_bias):
    B, S = q.shape[:2]
    pad = ((0, 0), (LEFT_FRAMES, 0), (0, 0), (0, 0))
    kp, vp = jnp.pad(k, pad), jnp.pad(v, pad)

    def one_chunk(c):
        s0 = c * CHUNK
        qb = lax.dynamic_slice_in_dim(q, s0, CHUNK, axis=1)
        kb = lax.dynamic_slice_in_dim(kp, s0, BAND, axis=1)
        vb = lax.dynamic_slice_in_dim(vp, s0, BAND, axis=1)
        q_pos = s0 + jnp.arange(CHUNK, dtype=jnp.int32)
        k_pos = s0 - LEFT_FRAMES + jnp.arange(BAND, dtype=jnp.int32)
        return chunk_core(qb, kb, vb, q_pos, k_pos, rel_bias)

    out = lax.map(one_chunk, jnp.arange(S // CHUNK, dtype=jnp.int32))
    return jnp.moveaxis(out, 0, 1).reshape(B, S, C_HEADS, C_HEAD_DIM)


def token_mixers(h, pos, cache, w):
    B, S, _ = h.shape
    u_glu, cq, ckv_raw, qkv_c, gate_logits = jnp.split(h @ w["w_in"], IN_SPLITS, axis=-1)
    if cache is None:
        conv_hist = jnp.zeros((B, CONV_WIDTH - 1, C_CONV), h.dtype)
    else:
        conv_hist, ckv_cache, krope_cache, ck_cache, cv_cache = cache
    y_a, conv_state = conv_branch(u_glu, conv_hist, w)
    q_nope, q_rope, ckv, k_rope = mla_project(cq, ckv_raw, pos, w)
    if cache is None:
        ckv_all, krope_all, k_pos = ckv, k_rope, pos
    else:
        past = ckv_cache.shape[1]
        ckv_all = jnp.concatenate([ckv_cache.astype(ckv.dtype), ckv], axis=1)
        krope_all = jnp.concatenate([krope_cache.astype(k_rope.dtype), k_rope], axis=1)
        k_pos = jnp.concatenate([jnp.arange(past, dtype=jnp.int32), pos])
    kv = (ckv_all @ w["kv_up"]).reshape(B, -1, MLA_HEADS, QK_NOPE + V_HEAD)
    k_nope, v = kv[..., :QK_NOPE], kv[..., QK_NOPE:]
    if cache is None:
        o_b = mla_blocked(q_nope, q_rope, pos, k_nope, krope_all, v, k_pos)
    else:
        o_b = mla_core(q_nope, q_rope, pos, k_nope, krope_all, v, k_pos)
    y_b = o_b.reshape(B, S, MLA_HEADS * V_HEAD) @ w["mla_o"]
    qc, kc, vc = [t.reshape(B, S, C_HEADS, C_HEAD_DIM) for t in jnp.split(qkv_c, 3, axis=-1)]
    if cache is None:
        o_c = chunk_band_prompt(qc, kc, vc, w["rel_bias"])
        keep = min(LEFT_FRAMES, S)
        ck_state, cv_state = kc[:, S - keep:], vc[:, S - keep:]
    else:
        keep = ck_cache.shape[1]
        k_all = jnp.concatenate([ck_cache.astype(kc.dtype), kc], axis=1)
        v_all = jnp.concatenate([cv_cache.astype(vc.dtype), vc], axis=1)
        k_pos_c = jnp.concatenate([past - keep + jnp.arange(keep, dtype=jnp.int32), pos])
        o_c = chunk_core(qc, k_all, v_all, pos, k_pos_c, w["rel_bias"])
        ck_state, cv_state = k_all[:, -keep:], v_all[:, -keep:]
    y_c = o_c.reshape(B, S, C_HEADS * C_HEAD_DIM) @ w["chunk_o"]
    g = jax.nn.sigmoid(gate_logits).reshape(B, S, N_BRANCH, D_MODEL)
    merged = g[:, :, 0] * y_a + g[:, :, 1] * y_b + g[:, :, 2] * y_c
    return merged @ w["w_o"], (ckv, k_rope, ck_state, cv_state, conv_state)


def trunk_layer(x, p_l, pos, cache, w):
    x = layer_norm(DN_ALPHA * x + 0.5 * swiglu(x, w["ffn1_gate"], w["ffn1_up"], w["ffn1_down"]),
                   w["ln1_g"], w["ln1_b"])
    mix, state = token_mixers(x, pos, cache, w)
    x = layer_norm(DN_ALPHA * x + mix, w["ln2_g"], w["ln2_b"])
    x = layer_norm(DN_ALPHA * x + 0.5 * swiglu(x, w["ffn2_gate"], w["ffn2_up"], w["ffn2_down"]),
                   w["ln3_g"], w["ln3_b"])
    x = x + jax.nn.sigmoid(x @ w["ple_gate"]) * (p_l @ w["ple_proj"])
    return x, state


def setup_inputs(seed: int = 0) -> dict:
    key = jax.random.key(seed)
    keys = iter(jax.random.split(key, 64))

    def nrm(shape, scale):
        return jax.random.normal(next(keys), shape, jnp.float32) * scale

    def gain(shape):
        return 1.0 + nrm(shape, 0.02)

    L_C = min(LEFT_FRAMES, PAST_LEN)
    return {
        "x_prompt": nrm((BATCH, SEQ, D_MODEL), 1.0),
        "x_sample": nrm((DEC_BATCH, DEC_SEQ, D_MODEL), 1.0),
        "cache_mla_ckv": nrm((DEPTH, DEC_BATCH, PAST_LEN, KV_LORA), 1.0),
        "cache_mla_krope": nrm((DEPTH, DEC_BATCH, PAST_LEN, QK_ROPE), 1.0),
        "cache_chunk_k": nrm((DEPTH, DEC_BATCH, L_C, C_HEADS, C_HEAD_DIM), 1.0),
        "cache_chunk_v": nrm((DEPTH, DEC_BATCH, L_C, C_HEADS, C_HEAD_DIM), 1.0),
        "state_conv": nrm((DEPTH, DEC_BATCH, CONV_WIDTH - 1, C_CONV), 0.5),
        "p_prompt": nrm((DEPTH, BATCH, SEQ, PLE_DIM), 1.0),
        "p_sample": nrm((DEPTH, DEC_BATCH, DEC_SEQ, PLE_DIM), 1.0),
        "ln1_g": gain((DEPTH, D_MODEL)),
        "ln1_b": nrm((DEPTH, D_MODEL), 0.02),
        "w_ffn1_gate": nrm((DEPTH, D_MODEL, D_FF), D_MODEL ** -0.5),
        "w_ffn1_up": nrm((DEPTH, D_MODEL, D_FF), D_MODEL ** -0.5),
        "w_ffn1_down": nrm((DEPTH, D_FF, D_MODEL), DN_BETA * D_FF ** -0.5),
        "w_in": nrm((DEPTH, D_MODEL, D_IN), D_MODEL ** -0.5),
        "conv_dw_w": nrm((DEPTH, CONV_WIDTH, C_CONV), CONV_WIDTH ** -0.5),
        "conv_dw_b": nrm((DEPTH, C_CONV), 0.02),
        "conv_ln_g": gain((DEPTH, C_CONV)),
        "conv_ln_b": nrm((DEPTH, C_CONV), 0.02),
        "w_conv_pw": nrm((DEPTH, C_CONV, D_MODEL), C_CONV ** -0.5),
        "mla_q_norm": gain((DEPTH, Q_LORA)),
        "w_q_up": nrm((DEPTH, Q_LORA, MLA_HEADS * (QK_NOPE + QK_ROPE)), Q_LORA ** -0.5),
        "mla_kv_norm": gain((DEPTH, KV_LORA)),
        "w_kv_up": nrm((DEPTH, KV_LORA, MLA_HEADS * (QK_NOPE + V_HEAD)), KV_LORA ** -0.5),
        "w_mla_o": nrm((DEPTH, MLA_HEADS * V_HEAD, D_MODEL), (MLA_HEADS * V_HEAD) ** -0.5),
        "rel_bias": nrm((DEPTH, C_HEADS, N_REL), 0.1),
        "w_chunk_o": nrm((DEPTH, C_HEADS * C_HEAD_DIM, D_MODEL), (C_HEADS * C_HEAD_DIM) ** -0.5),
        "w_o": nrm((DEPTH, D_MODEL, D_MODEL), DN_BETA * D_MODEL ** -0.5),
        "ln2_g": gain((DEPTH, D_MODEL)),
        "ln2_b": nrm((DEPTH, D_MODEL), 0.02),
        "w_ffn2_gate": nrm((DEPTH, D_MODEL, D_FF), D_MODEL ** -0.5),
        "w_ffn2_up": nrm((DEPTH, D_MODEL, D_FF), D_MODEL ** -0.5),
        "w_ffn2_down": nrm((DEPTH, D_FF, D_MODEL), DN_BETA * D_FF ** -0.5),
        "ln3_g": gain((DEPTH, D_MODEL)),
        "ln3_b": nrm((DEPTH, D_MODEL), 0.02),
        "w_ple_proj": nrm((DEPTH, PLE_DIM, D_MODEL), PLE_DIM ** -0.5),
        "w_ple_gate": nrm((DEPTH, D_MODEL, D_MODEL), D_MODEL ** -0.5),
    }


def reference(x_prompt, x_sample, cache_mla_ckv, cache_mla_krope, cache_chunk_k, cache_chunk_v,
              state_conv, p_prompt, p_sample, ln1_g, ln1_b, w_ffn1_gate, w_ffn1_up, w_ffn1_down,
              w_in, conv_dw_w, conv_dw_b, conv_ln_g, conv_ln_b, w_conv_pw, mla_q_norm, w_q_up,
              mla_kv_norm, w_kv_up, w_mla_o, rel_bias, w_chunk_o, w_o, ln2_g, ln2_b,
              w_ffn2_gate, w_ffn2_up, w_ffn2_down, ln3_g, ln3_b, w_ple_proj, w_ple_gate):
    past = cache_mla_ckv.shape[2]
    pos_prompt = jnp.arange(x_prompt.shape[1], dtype=jnp.int32)
    pos_sample = past + jnp.arange(x_sample.shape[1], dtype=jnp.int32)
    xp, xs = x_prompt, x_sample
    sp_all, ss_all = [], []
    for i in range(DEPTH):
        w = {
            "ln1_g": ln1_g[i], "ln1_b": ln1_b[i],
            "ffn1_gate": w_ffn1_gate[i], "ffn1_up": w_ffn1_up[i], "ffn1_down": w_ffn1_down[i],
            "w_in": w_in[i],
            "conv_dw_w": conv_dw_w[i], "conv_dw_b": conv_dw_b[i],
            "conv_ln_g": conv_ln_g[i], "conv_ln_b": conv_ln_b[i], "conv_pw": w_conv_pw[i],
            "mla_q_norm": mla_q_norm[i], "q_up": w_q_up[i], "mla_kv_norm": mla_kv_norm[i],
            "kv_up": w_kv_up[i], "mla_o": w_mla_o[i],
            "rel_bias": rel_bias[i], "chunk_o": w_chunk_o[i], "w_o": w_o[i],
            "ln2_g": ln2_g[i], "ln2_b": ln2_b[i],
            "ffn2_gate": w_ffn2_gate[i], "ffn2_up": w_ffn2_up[i], "ffn2_down": w_ffn2_down[i],
            "ln3_g": ln3_g[i], "ln3_b": ln3_b[i],
            "ple_proj": w_ple_proj[i], "ple_gate": w_ple_gate[i],
        }
        xp, sp = trunk_layer(xp, p_prompt[i], pos_prompt, None, w)
        xs, ss = trunk_layer(xs, p_sample[i], pos_sample,
                             (state_conv[i], cache_mla_ckv[i], cache_mla_krope[i],
                              cache_chunk_k[i], cache_chunk_v[i]), w)
        sp_all.append(sp)
        ss_all.append(ss)
    new_mla_ckv_prompt = jnp.stack([s[0] for s in sp_all])
    new_mla_krope_prompt = jnp.stack([s[1] for s in sp_all])
    new_chunk_k_prompt = jnp.stack([s[2] for s in sp_all])
    new_chunk_v_prompt = jnp.stack([s[3] for s in sp_all])
    new_conv_prompt = jnp.stack([s[4] for s in sp_all])
    new_mla_ckv_sample = jnp.stack([s[0] for s in ss_all])
    new_mla_krope_sample = jnp.stack([s[1] for s in ss_all])
    new_chunk_k_sample = jnp.stack([s[2] for s in ss_all])
    new_chunk_v_sample = jnp.stack([s[3] for s in ss_all])
    new_conv_sample = jnp.stack([s[4] for s in ss_all])
    return (xp, xs, new_mla_ckv_prompt, new_mla_krope_prompt, new_chunk_k_prompt, new_chunk_v_prompt,
            new_conv_prompt, new_mla_ckv_sample, new_mla_krope_sample, new_chunk_k_sample,
            new_chunk_v_sample, new_conv_sample)
```

```python
import functools

import jax
import jax.numpy as jnp
from jax import lax
from jax.experimental import pallas as pl
from jax.experimental.pallas import tpu as pltpu

BF = jnp.bfloat16
F32 = jnp.float32

D_MODEL = 2048
DEPTH = 4
CHUNK = 64
D_FF = 5632
C_CONV = 1024
CONV_WIDTH = 31
MLA_HEADS = 8
QK_NOPE = 128
QK_ROPE = 64
V_HEAD = 128
Q_LORA = 768
KV_LORA = 512
ROPE_THETA = 10000.0
MLA_SCALE = (QK_NOPE + QK_ROPE) ** -0.5
C_HEADS = 8
C_HEAD_DIM = 128
C_SCALE = C_HEAD_DIM ** -0.5
LEFT_CHUNKS = 8
LEFT_FRAMES = LEFT_CHUNKS * CHUNK
REL_MAX = 128
PLE_DIM = 256
DN_ALPHA = (2 * DEPTH) ** 0.25
LN_EPS = 1e-5
RMS_EPS = 1e-6
NEG_INF = -1e30

IN_CONV = 2 * C_CONV
IN_Q = Q_LORA
IN_KV = KV_LORA + QK_ROPE
IN_C = 3 * C_HEADS * C_HEAD_DIM
IN_GATE = 3 * D_MODEL

LANES = 128
HEAD_PAD = 256
HIST_PAD = 32
VMEM_LIMIT = 56 * 1024 * 1024


def _cparams(sem):
    return pltpu.CompilerParams(dimension_semantics=sem, vmem_limit_bytes=VMEM_LIMIT)


def _dot(a, b):
    return jnp.dot(a, b, preferred_element_type=F32)


def _dot_t(a, b):
    return lax.dot_general(a, b, (((1,), (1,)), ((), ())), preferred_element_type=F32)


def _layer_norm(y, g, b):
    mu = jnp.mean(y, axis=-1, keepdims=True)
    d = y - mu
    var = jnp.mean(d * d, axis=-1, keepdims=True)
    return d * lax.rsqrt(var + LN_EPS) * g + b


def _rms_norm(y, g):
    return y * lax.rsqrt(jnp.mean(y * y, axis=-1, keepdims=True) + RMS_EPS) * g


def _chunk_of(pos):
    return lax.shift_right_arithmetic(pos, CHUNK.bit_length() - 1)


def _rope128(blk, c, sa, sb):
    return blk * c + pltpu.roll(blk, LANES - QK_ROPE // 2, 1) * sa + pltpu.roll(blk, QK_ROPE // 2, 1) * sb


def _ffn_body(x_ref, wg_ref, wu_ref, wd_ref, g_ref, b_ref, *rest, nf, emit_bf):
    if emit_bf:
        o_ref, ob_ref, xb_ref, acc_ref = rest
    else:
        o_ref, xb_ref, acc_ref = rest
    j = pl.program_id(1)

    @pl.when(j == 0)
    def _():
        xb_ref[...] = x_ref[...].astype(BF)
        acc_ref[...] = jnp.zeros_like(acc_ref)

    xb = xb_ref[...]
    g = _dot(xb, wg_ref[...])
    u = _dot(xb, wu_ref[...])
    h = (g * jax.nn.sigmoid(g) * u).astype(BF)
    acc_ref[...] += _dot(h, wd_ref[...])

    @pl.when(j == nf - 1)
    def _():
        y = _layer_norm(DN_ALPHA * x_ref[...] + 0.5 * acc_ref[...], g_ref[...], b_ref[...])
        o_ref[...] = y
        if emit_bf:
            ob_ref[...] = y.astype(BF)


def _ffn_ln(x, wg, wu, wd, g, b, *, emit_bf):
    t, d = x.shape
    f = wg.shape[1]
    tm = min(512, t)
    tf = 512
    nf = f // tf
    out_shape = [jax.ShapeDtypeStruct((t, d), F32)]
    out_specs = [pl.BlockSpec((tm, d), lambda i, j: (i, 0))]
    if emit_bf:
        out_shape.append(jax.ShapeDtypeStruct((t, d), BF))
        out_specs.append(pl.BlockSpec((tm, d), lambda i, j: (i, 0)))
    res = pl.pallas_call(
        functools.partial(_ffn_body, nf=nf, emit_bf=emit_bf),
        grid=(t // tm, nf),
        in_specs=[
            pl.BlockSpec((tm, d), lambda i, j: (i, 0)),
            pl.BlockSpec((d, tf), lambda i, j: (0, j)),
            pl.BlockSpec((d, tf), lambda i, j: (0, j)),
            pl.BlockSpec((tf, d), lambda i, j: (j, 0)),
            pl.BlockSpec((1, d), lambda i, j: (0, 0)),
            pl.BlockSpec((1, d), lambda i, j: (0, 0)),
        ],
        out_specs=out_specs,
        out_shape=out_shape,
        scratch_shapes=[pltpu.VMEM((tm, d), BF), pltpu.VMEM((tm, d), F32)],
        compiler_params=_cparams(("parallel", "arbitrary")),
        name="ffn_ln",
    )(x, wg, wu, wd, g.reshape(1, d), b.reshape(1, d))
    return res if emit_bf else res[0]


def _glu_body(x_ref, w_ref, o_ref):
    r = _dot(x_ref[...], w_ref[...])
    o_ref[...] = r[:, :C_CONV] * jax.nn.sigmoid(r[:, C_CONV:])


def _glu_proj(hb, w):
    t, d = hb.shape
    tm = min(512, t)
    return pl.pallas_call(
        _glu_body,
        grid=(t // tm,),
        in_specs=[pl.BlockSpec((tm, d), lambda i: (i, 0)),
                  pl.BlockSpec((d, IN_CONV), lambda i: (0, 0))],
        out_specs=pl.BlockSpec((tm, C_CONV), lambda i: (i, 0)),
        out_shape=jax.ShapeDtypeStruct((t, C_CONV), F32),
        compiler_params=_cparams(("parallel",)),
        name="conv_glu_proj",
    )(hb, w)


def _mlaq_body(x_ref, wq_ref, gq_ref, wup_ref, c_ref, sa_ref, sb_ref, o_ref):
    cq = _dot(x_ref[...], wq_ref[...])
    n = _rms_norm(cq, gq_ref[...]).astype(BF)
    q = _dot(n, wup_ref[...])
    c, sa, sb = c_ref[...], sa_ref[...], sb_ref[...]
    for h in range(MLA_HEADS):
        lo = h * HEAD_PAD
        o_ref[:, lo:lo + QK_NOPE] = q[:, lo:lo + QK_NOPE].astype(BF)
        o_ref[:, lo + QK_NOPE:lo + HEAD_PAD] = _rope128(q[:, lo + QK_NOPE:lo + HEAD_PAD], c, sa, sb).astype(BF)


def _mla_q_proj(hb, wq, gq, wup, tabs, tm):
    t, d = hb.shape
    n_tab = tabs[0].shape[0] // tm
    tab_spec = pl.BlockSpec((tm, LANES), lambda i: (i % n_tab, 0))
    return pl.pallas_call(
        _mlaq_body,
        grid=(t // tm,),
        in_specs=[pl.BlockSpec((tm, d), lambda i: (i, 0)),
                  pl.BlockSpec((d, Q_LORA), lambda i: (0, 0)),
                  pl.BlockSpec((1, Q_LORA), lambda i: (0, 0)),
                  pl.BlockSpec((Q_LORA, MLA_HEADS * HEAD_PAD), lambda i: (0, 0)),
                  tab_spec, tab_spec, tab_spec],
        out_specs=pl.BlockSpec((tm, MLA_HEADS * HEAD_PAD), lambda i: (i, 0)),
        out_shape=jax.ShapeDtypeStruct((t, MLA_HEADS * HEAD_PAD), BF),
        compiler_params=_cparams(("parallel",)),
        name="mla_q_proj",
    )(hb, wq, gq.reshape(1, Q_LORA), wup, *tabs)


def _mlakv_body(x_ref, w_ref, g_ref, c_ref, sa_ref, sb_ref, ckv_ref, kr_ref, krp_ref):
    r = _dot(x_ref[...], w_ref[...])
    ckv_ref[...] = _rms_norm(r[:, :KV_LORA], g_ref[...])
    kr = _rope128(r[:, KV_LORA:], c_ref[...], sa_ref[...], sb_ref[...])
    krp_ref[...] = kr
    kr_ref[...] = kr[:, :QK_ROPE]


def _mla_kv_latent(hb, w, g, tabs, tm):
    t, d = hb.shape
    n_tab = tabs[0].shape[0] // tm
    tab_spec = pl.BlockSpec((tm, LANES), lambda i: (i % n_tab, 0))
    return pl.pallas_call(
        _mlakv_body,
        grid=(t // tm,),
        in_specs=[pl.BlockSpec((tm, d), lambda i: (i, 0)),
                  pl.BlockSpec((d, KV_LORA + LANES), lambda i: (0, 0)),
                  pl.BlockSpec((1, KV_LORA), lambda i: (0, 0)),
                  tab_spec, tab_spec, tab_spec],
        out_specs=[pl.BlockSpec((tm, KV_LORA), lambda i: (i, 0)),
                   pl.BlockSpec((tm, QK_ROPE), lambda i: (i, 0)),
                   pl.BlockSpec((tm, LANES), lambda i: (i, 0))],
        out_shape=[jax.ShapeDtypeStruct((t, KV_LORA), F32),
                   jax.ShapeDtypeStruct((t, QK_ROPE), F32),
                   jax.ShapeDtypeStruct((t, LANES), F32)],
        compiler_params=_cparams(("parallel",)),
        name="mla_kv_latent",
    )(hb, w, g.reshape(1, KV_LORA), *tabs)


def _kvup_body(ckv_ref, krp_ref, wkn_ref, wv_ref, kf_ref, v_ref):
    c = ckv_ref[...].astype(BF)
    kn = _dot(c, wkn_ref[...]).astype(BF)
    v_ref[...] = _dot(c, wv_ref[...]).astype(BF)
    krp = krp_ref[...].astype(BF)
    for h in range(MLA_HEADS):
        lo = h * HEAD_PAD
        kf_ref[:, lo:lo + QK_NOPE] = kn[:, h * QK_NOPE:(h + 1) * QK_NOPE]
        kf_ref[:, lo + QK_NOPE:lo + HEAD_PAD] = krp


def _mla_kv_up(ckv, krp, wkn, wv, tm):
    t = ckv.shape[0]
    return pl.pallas_call(
        _kvup_body,
        grid=(t // tm,),
        in_specs=[pl.BlockSpec((tm, KV_LORA), lambda i: (i, 0)),
                  pl.BlockSpec((tm, LANES), lambda i: (i, 0)),
                  pl.BlockSpec((KV_LORA, MLA_HEADS * QK_NOPE), lambda i: (0, 0)),
                  pl.BlockSpec((KV_LORA, MLA_HEADS * V_HEAD), lambda i: (0, 0))],
        out_specs=[pl.BlockSpec((tm, MLA_HEADS * HEAD_PAD), lambda i: (i, 0)),
                   pl.BlockSpec((tm, MLA_HEADS * V_HEAD), lambda i: (i, 0))],
        out_shape=[jax.ShapeDtypeStruct((t, MLA_HEADS * HEAD_PAD), BF),
                   jax.ShapeDtypeStruct((t, MLA_HEADS * V_HEAD), BF)],
        compiler_params=_cparams(("parallel",)),
        name="mla_kv_up",
    )(ckv, krp, wkn, wv)


def _cqkv_body(x_ref, w_ref, q_ref, k_ref, v_ref, ks_ref, vs_ref, *, first_state, tiles_per_seq):
    hc = C_HEADS * C_HEAD_DIM
    r = _dot(x_ref[...], w_ref[...])
    q_ref[...] = r[:, :hc].astype(BF)
    k = r[:, hc:2 * hc]
    v = r[:, 2 * hc:]
    k_ref[...] = k.astype(BF)
    v_ref[...] = v.astype(BF)
    s = pl.program_id(0) % tiles_per_seq
    tm = k.shape[0]

    @pl.when(s >= first_state)
    def _():
        off = pl.multiple_of((s - first_state) * tm, tm)
        ks_ref[pl.ds(off, tm), :] = k
        vs_ref[pl.ds(off, tm), :] = v


def _chunk_qkv_proj(hb, w, seq, keep, tm):
    t, d = hb.shape
    hc = C_HEADS * C_HEAD_DIM
    tiles_per_seq = seq // tm
    first_state = (seq - keep) // tm
    nb = t // seq
    full = pl.BlockSpec((tm, hc), lambda i: (i, 0))
    state = pl.BlockSpec((keep, hc), lambda i: (i // tiles_per_seq, 0))
    return pl.pallas_call(
        functools.partial(_cqkv_body, first_state=first_state, tiles_per_seq=tiles_per_seq),
        grid=(t // tm,),
        in_specs=[pl.BlockSpec((tm, d), lambda i: (i, 0)),
                  pl.BlockSpec((d, IN_C), lambda i: (0, 0))],
        out_specs=[full, full, full, state, state],
        out_shape=[jax.ShapeDtypeStruct((t, hc), BF)] * 3 + [jax.ShapeDtypeStruct((nb * keep, hc), F32)] * 2,
        compiler_params=_cparams(("arbitrary",)),
        name="chunk_qkv_proj",
    )(hb, w)


CONV_ROWS = 32
CONV_COLS = 256
LN_ROWS = 64


def _conv_body(ucur_ref, uprev_ref, hist_ref, w_ref, b_ref, g_ref, beta_ref, o_ref, xin_ref, y_ref, *, ts):
    s = pl.program_id(1)

    @pl.when(s == 0)
    def _():
        xin_ref[0:HIST_PAD, :] = hist_ref[0]

    @pl.when(s > 0)
    def _():
        xin_ref[0:HIST_PAD, :] = uprev_ref[0]

    xin_ref[HIST_PAD:HIST_PAD + ts, :] = ucur_ref[0]
    lead = HIST_PAD - (CONV_WIDTH - 1)
    for c0 in range(0, C_CONV, CONV_COLS):
        for r0 in range(0, ts, CONV_ROWS):
            acc = jnp.broadcast_to(b_ref[:, c0:c0 + CONV_COLS], (CONV_ROWS, CONV_COLS))
            for k in range(CONV_WIDTH):
                acc = acc + w_ref[k:k + 1, c0:c0 + CONV_COLS] * xin_ref[r0 + lead + k:r0 + lead + k + CONV_ROWS, c0:c0 + CONV_COLS]
            y_ref[r0:r0 + CONV_ROWS, c0:c0 + CONV_COLS] = acc
    g, beta = g_ref[...], beta_ref[...]
    for r0 in range(0, ts, LN_ROWS):
        rows = min(LN_ROWS, ts - r0)
        y = _layer_norm(y_ref[r0:r0 + rows, :], g, beta)
        o_ref[0, r0:r0 + rows, :] = (y * jax.nn.sigmoid(y)).astype(BF)


def _conv_branch(u, hist, w, b, g, beta, nb, seq):
    ts = min(256, seq)
    wp = jnp.pad(w, ((0, HIST_PAD - CONV_WIDTH), (0, 0)))
    u3 = u.reshape(nb, seq, C_CONV)
    per = ts // HIST_PAD
    row = lambda a: a.reshape(1, C_CONV)
    out = pl.pallas_call(
        functools.partial(_conv_body, ts=ts),
        grid=(nb, seq // ts),
        in_specs=[pl.BlockSpec((1, ts, C_CONV), lambda bi, s: (bi, s, 0)),
                  pl.BlockSpec((1, HIST_PAD, C_CONV), lambda bi, s: (bi, jnp.maximum(s * per - 1, 0), 0)),
                  pl.BlockSpec((1, HIST_PAD, C_CONV), lambda bi, s: (bi, 0, 0)),
                  pl.BlockSpec((HIST_PAD, C_CONV), lambda bi, s: (0, 0)),
                  pl.BlockSpec((1, C_CONV), lambda bi, s: (0, 0)),
                  pl.BlockSpec((1, C_CONV), lambda bi, s: (0, 0)),
                  pl.BlockSpec((1, C_CONV), lambda bi, s: (0, 0))],
        out_specs=pl.BlockSpec((1, ts, C_CONV), lambda bi, s: (bi, s, 0)),
        out_shape=jax.ShapeDtypeStruct((nb, seq, C_CONV), BF),
        scratch_shapes=[pltpu.VMEM((HIST_PAD + ts, C_CONV), F32), pltpu.VMEM((ts, C_CONV), F32)],
        compiler_params=_cparams(("parallel", "arbitrary")),
        name="conv_branch",
    )(u3, u3, hist, wp, row(b), row(g), row(beta))
    return out.reshape(nb * seq, C_CONV)


MLA_TQ = 512


def _mla_prompt_body(q_ref, k_ref, v_ref, o_ref, m_ref, l_ref, acc_ref, *, tq):
    qi = pl.program_id(2)
    q = q_ref[...]
    m_ref[...] = jnp.full_like(m_ref, NEG_INF)
    l_ref[...] = jnp.zeros_like(l_ref)
    acc_ref[...] = jnp.zeros_like(acc_ref)

    def step(ki, masked):
        off = pl.multiple_of(ki * tq, tq)
        s = _dot_t(q, k_ref[pl.ds(off, tq), :]) * MLA_SCALE
        if masked:
            rc = _chunk_of(lax.broadcasted_iota(jnp.int32, (tq, tq), 0))
            cc = _chunk_of(lax.broadcasted_iota(jnp.int32, (tq, tq), 1))
            s = jnp.where(cc <= rc, s, NEG_INF)
        m_old = m_ref[...]
        m_new = jnp.maximum(m_old, jnp.max(s, axis=-1, keepdims=True))
        alpha = jnp.exp(m_old - m_new)
        p = jnp.exp(s - m_new)
        l_ref[...] = alpha * l_ref[...] + jnp.sum(p, axis=-1, keepdims=True)
        acc_ref[...] = alpha * acc_ref[...] + _dot(p.astype(BF), v_ref[pl.ds(off, tq), :])
        m_ref[...] = m_new

    def body(ki, carry):
        step(ki, False)
        return carry

    lax.fori_loop(0, qi, body, 0)
    step(qi, True)
    o_ref[...] = (acc_ref[...] / l_ref[...]).astype(BF)


def _mla_attn_prompt(qf, kf, v, nb, seq):
    tq = min(MLA_TQ, seq)
    nq = seq // tq
    return pl.pallas_call(
        functools.partial(_mla_prompt_body, tq=tq),
        grid=(nb, MLA_HEADS, nq),
        in_specs=[pl.BlockSpec((tq, HEAD_PAD), lambda b, h, i: (b * nq + i, h)),
                  pl.BlockSpec((seq, HEAD_PAD), lambda b, h, i: (b, h)),
                  pl.BlockSpec((seq, V_HEAD), lambda b, h, i: (b, h))],
        out_specs=pl.BlockSpec((tq, V_HEAD), lambda b, h, i: (b * nq + i, h)),
        out_shape=jax.ShapeDtypeStruct((nb * seq, MLA_HEADS * V_HEAD), BF),
        scratch_shapes=[pltpu.VMEM((tq, 1), F32), pltpu.VMEM((tq, 1), F32), pltpu.VMEM((tq, V_HEAD), F32)],
        compiler_params=_cparams(("parallel", "parallel", "arbitrary")),
        name="mla_attn_prompt",
    )(qf, kf, v)


def _mla_sample_body(q_ref, k_ref, v_ref, o_ref, *, past, n_keys):
    nq = q_ref.shape[0]
    nk = k_ref.shape[0]
    s = _dot_t(q_ref[...], k_ref[...]) * MLA_SCALE
    qpos = past + lax.broadcasted_iota(jnp.int32, (nq, nk), 0)
    kpos = lax.broadcasted_iota(jnp.int32, (nq, nk), 1)
    s = jnp.where(kpos < n_keys, jnp.where(_chunk_of(kpos) <= _chunk_of(qpos), s, NEG_INF), NEG_INF)
    e = jnp.exp(s - jnp.max(s, axis=-1, keepdims=True))
    p = e / jnp.sum(e, axis=-1, keepdims=True)
    o_ref[...] = _dot(p.astype(BF), v_ref[...]).astype(BF)


def _mla_attn_sample(qf, kf, v, nb, nq, nk, past, n_keys):
    return pl.pallas_call(
        functools.partial(_mla_sample_body, past=past, n_keys=n_keys),
        grid=(nb, MLA_HEADS),
        in_specs=[pl.BlockSpec((nq, HEAD_PAD), lambda b, h: (b, h)),
                  pl.BlockSpec((nk, HEAD_PAD), lambda b, h: (b, h)),
                  pl.BlockSpec((nk, V_HEAD), lambda b, h: (b, h))],
        out_specs=pl.BlockSpec((nq, V_HEAD), lambda b, h: (b, h)),
        out_shape=jax.ShapeDtypeStruct((nb * nq, MLA_HEADS * V_HEAD), BF),
        compiler_params=_cparams(("parallel", "parallel")),
        name="mla_attn_sample",
    )(qf, kf, v)


BAND_TQ = 256
BAND_BLOCKS = LEFT_FRAMES // BAND_TQ + 1


def _band_prompt_body(q_ref, *refs):
    k_refs = refs[:BAND_BLOCKS]
    v_refs = refs[BAND_BLOCKS:2 * BAND_BLOCKS]
    bias_ref, o_ref = refs[2 * BAND_BLOCKS:]
    tq = q_ref.shape[0]
    for h in range(C_HEADS):
        sl = slice(h * C_HEAD_DIM, (h + 1) * C_HEAD_DIM)
        qh = q_ref[:, sl]
        s = jnp.concatenate([_dot_t(qh, kr[:, sl]) for kr in k_refs], axis=-1)
        s = s * C_SCALE + bias_ref[0, h]
        e = jnp.exp(s - jnp.max(s, axis=-1, keepdims=True))
        p = (e / jnp.sum(e, axis=-1, keepdims=True)).astype(BF)
        o = _dot(p[:, 0:tq], v_refs[0][:, sl])
        for j in range(1, BAND_BLOCKS):
            o = o + _dot(p[:, j * tq:(j + 1) * tq], v_refs[j][:, sl])
        o_ref[:, sl] = o.astype(BF)


def _band_attn_prompt(q, k, v, bias, nb, seq):
    tq = BAND_TQ
    nq = seq // tq
    hc = C_HEADS * C_HEAD_DIM
    back = BAND_BLOCKS - 1

    def kv_spec(j):
        return pl.BlockSpec((tq, hc), lambda b, i: (b * nq + jnp.maximum(i - (back - j), 0), 0))

    kv_specs = [kv_spec(j) for j in range(BAND_BLOCKS)]
    return pl.pallas_call(
        _band_prompt_body,
        grid=(nb, nq),
        in_specs=[pl.BlockSpec((tq, hc), lambda b, i: (b * nq + i, 0))] + kv_specs + kv_specs
        + [pl.BlockSpec((1, C_HEADS, tq, BAND_BLOCKS * tq), lambda b, i: (jnp.minimum(i, back), 0, 0, 0))],
        out_specs=pl.BlockSpec((tq, hc), lambda b, i: (b * nq + i, 0)),
        out_shape=jax.ShapeDtypeStruct((nb * seq, hc), BF),
        compiler_params=_cparams(("parallel", "arbitrary")),
        name="band_attn_prompt",
    )(q, *([k] * BAND_BLOCKS), *([v] * BAND_BLOCKS), bias)


def _band_sample_body(q_ref, k_ref, v_ref, bias_ref, o_ref):
    for h in range(C_HEADS):
        sl = slice(h * C_HEAD_DIM, (h + 1) * C_HEAD_DIM)
        s = _dot_t(q_ref[:, sl], k_ref[0, :, sl].astype(BF)) * C_SCALE + bias_ref[h]
        e = jnp.exp(s - jnp.max(s, axis=-1, keepdims=True))
        p = (e / jnp.sum(e, axis=-1, keepdims=True)).astype(BF)
        o_ref[:, sl] = _dot(p, v_ref[0, :, sl].astype(BF)).astype(BF)


def _band_attn_sample(q, k_all, v_all, bias, nb, nq):
    hc = C_HEADS * C_HEAD_DIM
    nk = k_all.shape[1]
    return pl.pallas_call(
        _band_sample_body,
        grid=(nb,),
        in_specs=[pl.BlockSpec((nq, hc), lambda b: (b, 0)),
                  pl.BlockSpec((1, nk, hc), lambda b: (b, 0, 0)),
                  pl.BlockSpec((1, nk, hc), lambda b: (b, 0, 0)),
                  pl.BlockSpec((C_HEADS, nq, nk), lambda b: (0, 0, 0))],
        out_specs=pl.BlockSpec((nq, hc), lambda b: (b, 0)),
        out_shape=jax.ShapeDtypeStruct((nb * nq, hc), BF),
        compiler_params=_cparams(("parallel",)),
        name="band_attn_sample",
    )(q, k_all, v_all, bias)


MERGE_NB = 256


def _merge_body(h_ref, hb_ref, ca_ref, ob_ref, oc_ref, wg0_ref, wg1_ref, wg2_ref, wpw_ref, wmo_ref, wco_ref,
                wo_ref, g_ref, b_ref, o_ref, acc_ref, *, nj):
    j = pl.program_id(1)

    @pl.when(j == 0)
    def _():
        acc_ref[...] = jnp.zeros_like(acc_ref)

    hb = hb_ref[...]
    merged = jax.nn.sigmoid(_dot(hb, wg0_ref[...])) * _dot(ca_ref[...], wpw_ref[...])
    merged = merged + jax.nn.sigmoid(_dot(hb, wg1_ref[...])) * _dot(ob_ref[...], wmo_ref[...])
    merged = merged + jax.nn.sigmoid(_dot(hb, wg2_ref[...])) * _dot(oc_ref[...], wco_ref[...])
    acc_ref[...] += _dot(merged.astype(BF), wo_ref[...])

    @pl.when(j == nj - 1)
    def _():
        o_ref[...] = _layer_norm(DN_ALPHA * h_ref[...] + acc_ref[...], g_ref[...], b_ref[...])


def _merge_ln(h, hb, ca, ob, oc, w_gate, wpw, wmo, wco, wo, g, b):
    t, d = h.shape
    tm = min(512, t)
    nbk = MERGE_NB
    nj = d // nbk
    tok = lambda width: pl.BlockSpec((tm, width), lambda i, j: (i, 0))
    gate = lambda br: pl.BlockSpec((d, nbk), lambda i, j: (0, br * nj + j))
    col = lambda rows: pl.BlockSpec((rows, nbk), lambda i, j: (0, j))
    vec = pl.BlockSpec((1, d), lambda i, j: (0, 0))
    return pl.pallas_call(
        functools.partial(_merge_body, nj=nj),
        grid=(t // tm, nj),
        in_specs=[tok(d), tok(d), tok(C_CONV), tok(MLA_HEADS * V_HEAD), tok(C_HEADS * C_HEAD_DIM),
                  gate(0), gate(1), gate(2),
                  col(C_CONV), col(MLA_HEADS * V_HEAD), col(C_HEADS * C_HEAD_DIM),
                  pl.BlockSpec((nbk, d), lambda i, j: (j, 0)), vec, vec],
        out_specs=pl.BlockSpec((tm, d), lambda i, j: (i, 0)),
        out_shape=jax.ShapeDtypeStruct((t, d), F32),
        scratch_shapes=[pltpu.VMEM((tm, d), F32)],
        compiler_params=_cparams(("parallel", "arbitrary")),
        name="merge_ln",
    )(h, hb, ca, ob, oc, w_gate, w_gate, w_gate, wpw, wmo, wco, wo, g.reshape(1, d), b.reshape(1, d))


def _ple_body(x_ref, p_ref, wg_ref, wp_ref, o_ref):
    x = x_ref[...]
    gate = jax.nn.sigmoid(_dot(x.astype(BF), wg_ref[...]))
    o_ref[...] = x + gate * _dot(p_ref[...].astype(BF), wp_ref[...])


def _ple(x, p, wg, wp):
    t, d = x.shape
    tm = min(512, t)
    return pl.pallas_call(
        _ple_body,
        grid=(t // tm,),
        in_specs=[pl.BlockSpec((tm, d), lambda i: (i, 0)),
                  pl.BlockSpec((tm, PLE_DIM), lambda i: (i, 0)),
                  pl.BlockSpec((d, d), lambda i: (0, 0)),
                  pl.BlockSpec((PLE_DIM, d), lambda i: (0, 0))],
        out_specs=pl.BlockSpec((tm, d), lambda i: (i, 0)),
        out_shape=jax.ShapeDtypeStruct((t, d), F32),
        compiler_params=_cparams(("parallel",)),
        name="ple",
    )(x, p, wg, wp)


def _rope_tables(pos):
    half = QK_ROPE // 2
    inv = ROPE_THETA ** (-jnp.arange(half, dtype=F32) / half)
    ang = pos.astype(F32)[:, None] * inv[None, :]
    cos, sin = jnp.cos(ang), jnp.sin(ang)
    z = jnp.zeros_like(cos)
    c = jnp.concatenate([cos, cos, z, z], axis=-1)
    sa = jnp.concatenate([-sin, z, z, z], axis=-1)
    sb = jnp.concatenate([z, sin, z, z], axis=-1)
    return c, sa, sb


def _band_bias(rel_bias, q_pos, k_pos, k_valid):
    rel = jnp.clip(q_pos[:, None] - k_pos[None, :], -REL_MAX, REL_MAX) + REL_MAX
    qc = q_pos[:, None] // CHUNK
    kc = k_pos[None, :] // CHUNK
    visible = (k_pos[None, :] >= 0) & (kc <= qc) & (kc >= qc - LEFT_CHUNKS) & k_valid[None, :]
    return jnp.where(visible[None], rel_bias[:, rel].astype(F32), NEG_INF)


def _layer_weights(i, a):
    bf = lambda w: w.astype(BF)
    w_in = a["w_in"][i]
    c0, c1, c2, c3 = IN_CONV, IN_CONV + IN_Q, IN_CONV + IN_Q + IN_KV, IN_CONV + IN_Q + IN_KV + IN_C
    q_up = a["w_q_up"][i].reshape(Q_LORA, MLA_HEADS, QK_NOPE + QK_ROPE)
    q_up = jnp.pad(q_up, ((0, 0), (0, 0), (0, HEAD_PAD - QK_NOPE - QK_ROPE))).reshape(Q_LORA, MLA_HEADS * HEAD_PAD)
    kv_up = a["w_kv_up"][i].reshape(KV_LORA, MLA_HEADS, QK_NOPE + V_HEAD)
    return dict(
        ln1_g=a["ln1_g"][i], ln1_b=a["ln1_b"][i], ln2_g=a["ln2_g"][i], ln2_b=a["ln2_b"][i],
        ln3_g=a["ln3_g"][i], ln3_b=a["ln3_b"][i],
        ffn1=(bf(a["w_ffn1_gate"][i]), bf(a["w_ffn1_up"][i]), bf(a["w_ffn1_down"][i])),
        ffn2=(bf(a["w_ffn2_gate"][i]), bf(a["w_ffn2_up"][i]), bf(a["w_ffn2_down"][i])),
        w_glu=bf(w_in[:, :c0]),
        w_cq=bf(w_in[:, c0:c1]),
        w_ckv=bf(jnp.pad(w_in[:, c1:c2], ((0, 0), (0, LANES - QK_ROPE)))),
        w_cqkv=bf(w_in[:, c2:c3]),
        w_gate=bf(w_in[:, c3:]),
        conv_w=a["conv_dw_w"][i], conv_b=a["conv_dw_b"][i], conv_g=a["conv_ln_g"][i], conv_beta=a["conv_ln_b"][i],
        w_pw=bf(a["w_conv_pw"][i]),
        q_norm=a["mla_q_norm"][i], q_up=bf(q_up), kv_norm=a["mla_kv_norm"][i],
        w_kn=bf(kv_up[:, :, :QK_NOPE].reshape(KV_LORA, MLA_HEADS * QK_NOPE)),
        w_v=bf(kv_up[:, :, QK_NOPE:].reshape(KV_LORA, MLA_HEADS * V_HEAD)),
        w_mo=bf(a["w_mla_o"][i]), rel_bias=a["rel_bias"][i], w_co=bf(a["w_chunk_o"][i]), w_o=bf(a["w_o"][i]),
        ple_gate=bf(a["w_ple_gate"][i]), ple_proj=bf(a["w_ple_proj"][i]),
    )


def _layer(x, p_l, w, nb, seq, tabs, bias, cache):
    t = nb * seq
    tm = min(512, t)
    h, hb = _ffn_ln(x, *w["ffn1"], w["ln1_g"], w["ln1_b"], emit_bf=True)

    u = _glu_proj(hb, w["w_glu"])
    if cache is None:
        hist = jnp.zeros((nb, HIST_PAD, C_CONV), F32)
    else:
        hist = jnp.pad(cache["conv"], ((0, 0), (HIST_PAD - (CONV_WIDTH - 1), 0), (0, 0)))
    ca = _conv_branch(u, hist, w["conv_w"], w["conv_b"], w["conv_g"], w["conv_beta"], nb, seq)
    u3 = u.reshape(nb, seq, C_CONV)
    if seq >= CONV_WIDTH - 1:
        conv_state = u3[:, seq - (CONV_WIDTH - 1):]
    else:
        conv_state = jnp.concatenate([cache["conv"], u3], axis=1)[:, -(CONV_WIDTH - 1):]

    qf = _mla_q_proj(hb, w["w_cq"], w["q_norm"], w["q_up"], tabs, tm)
    ckv, k_rope, krp = _mla_kv_latent(hb, w["w_ckv"], w["kv_norm"], tabs, tm)
    if cache is None:
        kf, v = _mla_kv_up(ckv, krp, w["w_kn"], w["w_v"], tm)
        o_b = _mla_attn_prompt(qf, kf, v, nb, seq)
    else:
        past = cache["ckv"].shape[1]
        n_keys = past + seq
        nk = -(-n_keys // LANES) * LANES
        tail = ((0, 0), (0, nk - n_keys), (0, 0))
        ckv_all = jnp.pad(jnp.concatenate([cache["ckv"], ckv.reshape(nb, seq, KV_LORA)], axis=1), tail)
        krope_c = jnp.pad(cache["krope"], ((0, 0), (0, 0), (0, LANES - QK_ROPE)))
        krp_all = jnp.pad(jnp.concatenate([krope_c, krp.reshape(nb, seq, LANES)], axis=1), tail)
        kf, v = _mla_kv_up(ckv_all.reshape(nb * nk, KV_LORA), krp_all.reshape(nb * nk, LANES),
                           w["w_kn"], w["w_v"], nk)
        o_b = _mla_attn_sample(qf, kf, v, nb, seq, nk, past, n_keys)

    hc = C_HEADS * C_HEAD_DIM
    if cache is None:
        keep = min(LEFT_FRAMES, seq)
        qc, kc, vc, ks, vs = _chunk_qkv_proj(hb, w["w_cqkv"], seq, keep, min(tm, keep))
        o_c = _band_attn_prompt(qc, kc, vc, bias, nb, seq)
        ck_state = ks.reshape(nb, keep, C_HEADS, C_HEAD_DIM)
        cv_state = vs.reshape(nb, keep, C_HEADS, C_HEAD_DIM)
    else:
        keep = cache["ck"].shape[1]
        qc, _, _, ks, vs = _chunk_qkv_proj(hb, w["w_cqkv"], seq, seq, seq)
        n_keys = keep + seq
        nk = -(-n_keys // LANES) * LANES
        tail = ((0, 0), (0, nk - n_keys), (0, 0))
        k_all = jnp.concatenate([cache["ck"].reshape(nb, keep, hc), ks.reshape(nb, seq, hc)], axis=1)
        v_all = jnp.concatenate([cache["cv"].reshape(nb, keep, hc), vs.reshape(nb, seq, hc)], axis=1)
        o_c = _band_attn_sample(qc, jnp.pad(k_all, tail), jnp.pad(v_all, tail), bias, nb, seq)
        ck_state = k_all[:, -keep:].reshape(nb, keep, C_HEADS, C_HEAD_DIM)
        cv_state = v_all[:, -keep:].reshape(nb, keep, C_HEADS, C_HEAD_DIM)

    x = _merge_ln(h, hb, ca, o_b, o_c, w["w_gate"], w["w_pw"], w["w_mo"], w["w_co"], w["w_o"],
                  w["ln2_g"], w["ln2_b"])
    x = _ffn_ln(x, *w["ffn2"], w["ln3_g"], w["ln3_b"], emit_bf=False)
    x = _ple(x, p_l, w["ple_gate"], w["ple_proj"])
    state = (ckv.reshape(nb, seq, KV_LORA), k_rope.reshape(nb, seq, QK_ROPE), ck_state, cv_state, conv_state)
    return x, state


def kernel(x_prompt, x_sample, cache_mla_ckv, cache_mla_krope, cache_chunk_k, cache_chunk_v, state_conv, p_prompt, p_sample, ln1_g, ln1_b, w_ffn1_gate, w_ffn1_up, w_ffn1_down, w_in, conv_dw_w, conv_dw_b, conv_ln_g, conv_ln_b, w_conv_pw, mla_q_norm, w_q_up, mla_kv_norm, w_kv_up, w_mla_o, rel_bias, w_chunk_o, w_o, ln2_g, ln2_b, w_ffn2_gate, w_ffn2_up, w_ffn2_down, ln3_g, ln3_b, w_ple_proj, w_ple_gate):
    a = dict(ln1_g=ln1_g, ln1_b=ln1_b, w_ffn1_gate=w_ffn1_gate, w_ffn1_up=w_ffn1_up, w_ffn1_down=w_ffn1_down,
             w_in=w_in, conv_dw_w=conv_dw_w, conv_dw_b=conv_dw_b, conv_ln_g=conv_ln_g, conv_ln_b=conv_ln_b,
             w_conv_pw=w_conv_pw, mla_q_norm=mla_q_norm, w_q_up=w_q_up, mla_kv_norm=mla_kv_norm,
             w_kv_up=w_kv_up, w_mla_o=w_mla_o, rel_bias=rel_bias, w_chunk_o=w_chunk_o, w_o=w_o,
             ln2_g=ln2_g, ln2_b=ln2_b, w_ffn2_gate=w_ffn2_gate, w_ffn2_up=w_ffn2_up, w_ffn2_down=w_ffn2_down,
             ln3_g=ln3_g, ln3_b=ln3_b, w_ple_proj=w_ple_proj, w_ple_gate=w_ple_gate)
    depth = w_in.shape[0]
    bp, sp, d = x_prompt.shape
    bs, ss, _ = x_sample.shape
    past = cache_mla_ckv.shape[2]
    keep_s = cache_chunk_k.shape[2]
    assert sp % BAND_TQ == 0 and sp % MLA_TQ == 0 and LEFT_FRAMES % BAND_TQ == 0

    pos_p = jnp.arange(sp, dtype=jnp.int32)
    pos_s = past + jnp.arange(ss, dtype=jnp.int32)
    tabs_p = _rope_tables(pos_p)
    tabs_s = tuple(jnp.tile(tb, (bs, 1)) for tb in _rope_tables(pos_s))

    nk_s = -(-(keep_s + ss) // LANES) * LANES
    kpos_s = past - keep_s + jnp.arange(nk_s, dtype=jnp.int32)
    kvalid_s = jnp.arange(nk_s) < keep_s + ss

    xp = x_prompt.reshape(bp * sp, d)
    xs = x_sample.reshape(bs * ss, d)
    sp_all, ss_all = [], []
    for i in range(depth):
        w = _layer_weights(i, a)
        all_valid = jnp.ones((BAND_BLOCKS * BAND_TQ,), jnp.bool_)
        bias_p = jnp.stack([
            _band_bias(w["rel_bias"], v * BAND_TQ + jnp.arange(BAND_TQ, dtype=jnp.int32),
                       v * BAND_TQ - LEFT_FRAMES + jnp.arange(BAND_BLOCKS * BAND_TQ, dtype=jnp.int32), all_valid)
            for v in range(BAND_BLOCKS)])
        bias_s = _band_bias(w["rel_bias"], pos_s, kpos_s, kvalid_s)
        xp, st_p = _layer(xp, p_prompt[i].reshape(bp * sp, PLE_DIM), w, bp, sp, tabs_p, bias_p, None)
        cache = dict(conv=state_conv[i], ckv=cache_mla_ckv[i], krope=cache_mla_krope[i],
                     ck=cache_chunk_k[i], cv=cache_chunk_v[i])
        xs, st_s = _layer(xs, p_sample[i].reshape(bs * ss, PLE_DIM), w, bs, ss, tabs_s, bias_s, cache)
        sp_all.append(st_p)
        ss_all.append(st_s)
    outs = [xp.reshape(bp, sp, d), xs.reshape(bs, ss, d)]
    for states in (sp_all, ss_all):
        for k in range(5):
            outs.append(jnp.stack([s[k] for s in states]))
    return tuple(outs)
```

```python
import functools

import jax
import jax.numpy as jnp
from jax import lax
from jax.experimental import pallas as pl
from jax.experimental.pallas import tpu as pltpu

BF = jnp.bfloat16
F32 = jnp.float32

D_MODEL = 2048
DEPTH = 4
CHUNK = 64
D_FF = 5632
C_CONV = 1024
CONV_WIDTH = 31
MLA_HEADS = 8
QK_NOPE = 128
QK_ROPE = 64
V_HEAD = 128
Q_LORA = 768
KV_LORA = 512
ROPE_THETA = 10000.0
MLA_SCALE = (QK_NOPE + QK_ROPE) ** -0.5
C_HEADS = 8
C_HEAD_DIM = 128
C_SCALE = C_HEAD_DIM ** -0.5
LEFT_CHUNKS = 8
LEFT_FRAMES = LEFT_CHUNKS * CHUNK
REL_MAX = 128
PLE_DIM = 256
DN_ALPHA = (2 * DEPTH) ** 0.25
LN_EPS = 1e-5
RMS_EPS = 1e-6
NEG_INF = -1e30

IN_CONV = 2 * C_CONV
IN_Q = Q_LORA
IN_KV = KV_LORA + QK_ROPE
IN_C = 3 * C_HEADS * C_HEAD_DIM
IN_GATE = 3 * D_MODEL

LANES = 128
SUBLANES = 8
HEAD_PAD = 256
HIST_PAD = 32
VMEM_LIMIT = 56 * 1024 * 1024


def _cparams(sem):
    return pltpu.CompilerParams(dimension_semantics=sem, vmem_limit_bytes=VMEM_LIMIT)


def _dot(a, b):
    return jnp.dot(a, b, preferred_element_type=F32)


def _dot_t(a, b):
    return lax.dot_general(a, b, (((1,), (1,)), ((), ())), preferred_element_type=F32)


def _layer_norm(y, g, b):
    mu = jnp.mean(y, axis=-1, keepdims=True)
    d = y - mu
    var = jnp.mean(d * d, axis=-1, keepdims=True)
    return d * lax.rsqrt(var + LN_EPS) * g + b


def _rms_norm(y, g):
    return y * lax.rsqrt(jnp.mean(y * y, axis=-1, keepdims=True) + RMS_EPS) * g


def _chunk_of(pos):
    return lax.shift_right_arithmetic(pos, CHUNK.bit_length() - 1)


def _rope128(blk, c, sa, sb):
    return blk * c + pltpu.roll(blk, LANES - QK_ROPE // 2, 1) * sa + pltpu.roll(blk, QK_ROPE // 2, 1) * sb


def _ffn_body(x_ref, wg_ref, wu_ref, wd_ref, g_ref, b_ref, *rest, nf, emit_bf):
    if emit_bf:
        o_ref, ob_ref, xb_ref, acc_ref = rest
    else:
        o_ref, xb_ref, acc_ref = rest
    j = pl.program_id(1)

    @pl.when(j == 0)
    def _():
        xb_ref[...] = x_ref[...].astype(BF)
        acc_ref[...] = jnp.zeros_like(acc_ref)

    xb = xb_ref[...]
    g = _dot(xb, wg_ref[...])
    u = _dot(xb, wu_ref[...])
    h = (g * jax.nn.sigmoid(g) * u).astype(BF)
    acc_ref[...] += _dot(h, wd_ref[...])

    @pl.when(j == nf - 1)
    def _():
        y = _layer_norm(DN_ALPHA * x_ref[...] + 0.5 * acc_ref[...], g_ref[...], b_ref[...])
        o_ref[...] = y
        if emit_bf:
            ob_ref[...] = y.astype(BF)


def _ffn_ln(x, wg, wu, wd, g, b, *, emit_bf):
    t, d = x.shape
    f = wg.shape[1]
    tm = min(512, t)
    tf = 512
    nf = f // tf
    out_shape = [jax.ShapeDtypeStruct((t, d), F32)]
    out_specs = [pl.BlockSpec((tm, d), lambda i, j: (i, 0))]
    if emit_bf:
        out_shape.append(jax.ShapeDtypeStruct((t, d), BF))
        out_specs.append(pl.BlockSpec((tm, d), lambda i, j: (i, 0)))
    res = pl.pallas_call(
        functools.partial(_ffn_body, nf=nf, emit_bf=emit_bf),
        grid=(t // tm, nf),
        in_specs=[
            pl.BlockSpec((tm, d), lambda i, j: (i, 0)),
            pl.BlockSpec((d, tf), lambda i, j: (0, j)),
            pl.BlockSpec((d, tf), lambda i, j: (0, j)),
            pl.BlockSpec((tf, d), lambda i, j: (j, 0)),
            pl.BlockSpec((1, d), lambda i, j: (0, 0)),
            pl.BlockSpec((1, d), lambda i, j: (0, 0)),
        ],
        out_specs=out_specs,
        out_shape=out_shape,
        scratch_shapes=[pltpu.VMEM((tm, d), BF), pltpu.VMEM((tm, d), F32)],
        compiler_params=_cparams(("parallel", "arbitrary")),
        name="ffn_ln",
    )(x, wg, wu, wd, g.reshape(1, d), b.reshape(1, d))
    return res if emit_bf else res[0]


def _glu_body(x_ref, w_ref, o_ref):
    r = _dot(x_ref[...], w_ref[...])
    o_ref[...] = r[:, :C_CONV] * jax.nn.sigmoid(r[:, C_CONV:])


def _glu_proj(hb, w):
    t, d = hb.shape
    tm = min(512, t)
    return pl.pallas_call(
        _glu_body,
        grid=(t // tm,),
        in_specs=[pl.BlockSpec((tm, d), lambda i: (i, 0)),
                  pl.BlockSpec((d, IN_CONV), lambda i: (0, 0))],
        out_specs=pl.BlockSpec((tm, C_CONV), lambda i: (i, 0)),
        out_shape=jax.ShapeDtypeStruct((t, C_CONV), F32),
        compiler_params=_cparams(("parallel",)),
        name="conv_glu_proj",
    )(hb, w)


def _mlaq_body(x_ref, wq_ref, gq_ref, wup_ref, c_ref, sa_ref, sb_ref, o_ref):
    cq = _dot(x_ref[...], wq_ref[...])
    n = _rms_norm(cq, gq_ref[...]).astype(BF)
    q = _dot(n, wup_ref[...])
    c, sa, sb = c_ref[...], sa_ref[...], sb_ref[...]
    for h in range(MLA_HEADS):
        lo = h * HEAD_PAD
        o_ref[:, lo:lo + QK_NOPE] = q[:, lo:lo + QK_NOPE].astype(BF)
        o_ref[:, lo + QK_NOPE:lo + HEAD_PAD] = _rope128(q[:, lo + QK_NOPE:lo + HEAD_PAD], c, sa, sb).astype(BF)


def _mla_q_proj(hb, wq, gq, wup, tabs, tm):
    t, d = hb.shape
    n_tab = tabs[0].shape[0] // tm
    tab_spec = pl.BlockSpec((tm, LANES), lambda i: (i % n_tab, 0))
    return pl.pallas_call(
        _mlaq_body,
        grid=(t // tm,),
        in_specs=[pl.BlockSpec((tm, d), lambda i: (i, 0)),
                  pl.BlockSpec((d, Q_LORA), lambda i: (0, 0)),
                  pl.BlockSpec((1, Q_LORA), lambda i: (0, 0)),
                  pl.BlockSpec((Q_LORA, MLA_HEADS * HEAD_PAD), lambda i: (0, 0)),
                  tab_spec, tab_spec, tab_spec],
        out_specs=pl.BlockSpec((tm, MLA_HEADS * HEAD_PAD), lambda i: (i, 0)),
        out_shape=jax.ShapeDtypeStruct((t, MLA_HEADS * HEAD_PAD), BF),
        compiler_params=_cparams(("parallel",)),
        name="mla_q_proj",
    )(hb, wq, gq.reshape(1, Q_LORA), wup, *tabs)


def _mlakv_body(x_ref, w_ref, g_ref, c_ref, sa_ref, sb_ref, ckv_ref, kr_ref, krp_ref):
    r = _dot(x_ref[...], w_ref[...])
    ckv_ref[...] = _rms_norm(r[:, :KV_LORA], g_ref[...])
    kr = _rope128(r[:, KV_LORA:], c_ref[...], sa_ref[...], sb_ref[...])
    krp_ref[...] = kr
    kr_ref[...] = kr[:, :QK_ROPE]


def _mla_kv_latent(hb, w, g, tabs, tm):
    t, d = hb.shape
    n_tab = tabs[0].shape[0] // tm
    tab_spec = pl.BlockSpec((tm, LANES), lambda i: (i % n_tab, 0))
    return pl.pallas_call(
        _mlakv_body,
        grid=(t // tm,),
        in_specs=[pl.BlockSpec((tm, d), lambda i: (i, 0)),
                  pl.BlockSpec((d, KV_LORA + LANES), lambda i: (0, 0)),
                  pl.BlockSpec((1, KV_LORA), lambda i: (0, 0)),
                  tab_spec, tab_spec, tab_spec],
        out_specs=[pl.BlockSpec((tm, KV_LORA), lambda i: (i, 0)),
                   pl.BlockSpec((tm, QK_ROPE), lambda i: (i, 0)),
                   pl.BlockSpec((tm, LANES), lambda i: (i, 0))],
        out_shape=[jax.ShapeDtypeStruct((t, KV_LORA), F32),
                   jax.ShapeDtypeStruct((t, QK_ROPE), F32),
                   jax.ShapeDtypeStruct((t, LANES), F32)],
        compiler_params=_cparams(("parallel",)),
        name="mla_kv_latent",
    )(hb, w, g.reshape(1, KV_LORA), *tabs)


def _kvup_body(ckv_ref, krp_ref, wkn_ref, wv_ref, kf_ref, v_ref):
    c = ckv_ref[...].astype(BF)
    kn = _dot(c, wkn_ref[...]).astype(BF)
    v_ref[...] = _dot(c, wv_ref[...]).astype(BF)
    krp = krp_ref[...].astype(BF)
    for h in range(MLA_HEADS):
        lo = h * HEAD_PAD
        kf_ref[:, lo:lo + QK_NOPE] = kn[:, h * QK_NOPE:(h + 1) * QK_NOPE]
        kf_ref[:, lo + QK_NOPE:lo + HEAD_PAD] = krp


def _mla_kv_up(ckv, krp, wkn, wv, tm):
    t = ckv.shape[0]
    return pl.pallas_call(
        _kvup_body,
        grid=(t // tm,),
        in_specs=[pl.BlockSpec((tm, KV_LORA), lambda i: (i, 0)),
                  pl.BlockSpec((tm, LANES), lambda i: (i, 0)),
                  pl.BlockSpec((KV_LORA, MLA_HEADS * QK_NOPE), lambda i: (0, 0)),
                  pl.BlockSpec((KV_LORA, MLA_HEADS * V_HEAD), lambda i: (0, 0))],
        out_specs=[pl.BlockSpec((tm, MLA_HEADS * HEAD_PAD), lambda i: (i, 0)),
                   pl.BlockSpec((tm, MLA_HEADS * V_HEAD), lambda i: (i, 0))],
        out_shape=[jax.ShapeDtypeStruct((t, MLA_HEADS * HEAD_PAD), BF),
                   jax.ShapeDtypeStruct((t, MLA_HEADS * V_HEAD), BF)],
        compiler_params=_cparams(("parallel",)),
        name="mla_kv_up",
    )(ckv, krp, wkn, wv)


def _cqkv_body(x_ref, w_ref, q_ref, k_ref, v_ref, ks_ref, vs_ref, *, first_state, tiles_per_seq):
    hc = C_HEADS * C_HEAD_DIM
    r = _dot(x_ref[...], w_ref[...])
    q_ref[...] = r[:, :hc].astype(BF)
    k = r[:, hc:2 * hc]
    v = r[:, 2 * hc:]
    k_ref[...] = k.astype(BF)
    v_ref[...] = v.astype(BF)
    s = pl.program_id(0) % tiles_per_seq
    tm = k.shape[0]

    @pl.when(s >= first_state)
    def _():
        off = pl.multiple_of((s - first_state) * tm, tm)
        ks_ref[pl.ds(off, tm), :] = k
        vs_ref[pl.ds(off, tm), :] = v


def _chunk_qkv_proj(hb, w, seq, keep, tm):
    t, d = hb.shape
    hc = C_HEADS * C_HEAD_DIM
    tiles_per_seq = seq // tm
    first_state = (seq - keep) // tm
    nb = t // seq
    full = pl.BlockSpec((tm, hc), lambda i: (i, 0))
    state = pl.BlockSpec((keep, hc), lambda i: (i // tiles_per_seq, 0))
    return pl.pallas_call(
        functools.partial(_cqkv_body, first_state=first_state, tiles_per_seq=tiles_per_seq),
        grid=(t // tm,),
        in_specs=[pl.BlockSpec((tm, d), lambda i: (i, 0)),
                  pl.BlockSpec((d, IN_C), lambda i: (0, 0))],
        out_specs=[full, full, full, state, state],
        out_shape=[jax.ShapeDtypeStruct((t, hc), BF)] * 3 + [jax.ShapeDtypeStruct((nb * keep, hc), F32)] * 2,
        compiler_params=_cparams(("arbitrary",)),
        name="chunk_qkv_proj",
    )(hb, w)


CONV_ROWS = 32
CONV_COLS = 256
LN_ROWS = 64


def _conv_body(ucur_ref, uprev_ref, hist_ref, w_ref, b_ref, g_ref, beta_ref, o_ref, xs_ref, y_ref, *, ts):
    s = pl.program_id(1)

    @pl.when(s == 0)
    def _():
        xs_ref[0, 0:HIST_PAD, :] = hist_ref[0]

    @pl.when(s > 0)
    def _():
        xs_ref[0, 0:HIST_PAD, :] = uprev_ref[0]

    xs_ref[0, HIST_PAD:HIST_PAD + ts, :] = ucur_ref[0]
    n_al = HIST_PAD + ts - SUBLANES
    for j in range(1, SUBLANES):
        xs_ref[j, 0:n_al, :] = xs_ref[0, j:j + n_al, :]
    lead = HIST_PAD - (CONV_WIDTH - 1)
    for c0 in range(0, C_CONV, CONV_COLS):
        cols = slice(c0, c0 + CONV_COLS)
        for r0 in range(0, ts, CONV_ROWS):
            acc = jnp.broadcast_to(b_ref[:, cols], (CONV_ROWS, CONV_COLS))
            for k in range(CONV_WIDTH):
                j = (lead + k) % SUBLANES
                base = r0 + lead + k - j
                acc = acc + w_ref[k:k + 1, cols] * xs_ref[j, base:base + CONV_ROWS, cols]
            y_ref[r0:r0 + CONV_ROWS, cols] = acc
    g, beta = g_ref[...], beta_ref[...]
    for r0 in range(0, ts, LN_ROWS):
        rows = min(LN_ROWS, ts - r0)
        y = _layer_norm(y_ref[r0:r0 + rows, :], g, beta)
        o_ref[0, r0:r0 + rows, :] = (y * jax.nn.sigmoid(y)).astype(BF)


def _conv_branch(u, hist, w, b, g, beta, nb, seq):
    ts = min(256, seq)
    wp = jnp.pad(w, ((0, HIST_PAD - CONV_WIDTH), (0, 0)))
    u3 = u.reshape(nb, seq, C_CONV)
    per = ts // HIST_PAD
    row = lambda a: a.reshape(1, C_CONV)
    out = pl.pallas_call(
        functools.partial(_conv_body, ts=ts),
        grid=(nb, seq // ts),
        in_specs=[pl.BlockSpec((1, ts, C_CONV), lambda bi, s: (bi, s, 0)),
                  pl.BlockSpec((1, HIST_PAD, C_CONV), lambda bi, s: (bi, jnp.maximum(s * per - 1, 0), 0)),
                  pl.BlockSpec((1, HIST_PAD, C_CONV), lambda bi, s: (bi, 0, 0)),
                  pl.BlockSpec((HIST_PAD, C_CONV), lambda bi, s: (0, 0)),
                  pl.BlockSpec((1, C_CONV), lambda bi, s: (0, 0)),
                  pl.BlockSpec((1, C_CONV), lambda bi, s: (0, 0)),
                  pl.BlockSpec((1, C_CONV), lambda bi, s: (0, 0))],
        out_specs=pl.BlockSpec((1, ts, C_CONV), lambda bi, s: (bi, s, 0)),
        out_shape=jax.ShapeDtypeStruct((nb, seq, C_CONV), BF),
        scratch_shapes=[pltpu.VMEM((SUBLANES, HIST_PAD + ts, C_CONV), F32), pltpu.VMEM((ts, C_CONV), F32)],
        compiler_params=_cparams(("parallel", "arbitrary")),
        name="conv_branch",
    )(u3, u3, hist, wp, row(b), row(g), row(beta))
    return out.reshape(nb * seq, C_CONV)


MLA_TQ = 512
MLA_HPS = 4
MLA_EXP2_SCALE = MLA_SCALE * 1.4426950408889634


def _mla_prompt_body(q_ref, k_ref, v_ref, o_ref, m_ref, l_ref, acc_ref, *, tq):
    qi = pl.program_id(2)
    m_ref[...] = jnp.full_like(m_ref, NEG_INF)
    l_ref[...] = jnp.zeros_like(l_ref)
    acc_ref[...] = jnp.zeros_like(acc_ref)

    def step(ki, masked):
        off = pl.multiple_of(ki * tq, tq)
        for hh in range(MLA_HPS):
            qk = slice(hh * HEAD_PAD, (hh + 1) * HEAD_PAD)
            s = _dot_t(q_ref[:, qk], k_ref[pl.ds(off, tq), qk])
            if masked:
                rc = _chunk_of(lax.broadcasted_iota(jnp.int32, (tq, tq), 0))
                cc = _chunk_of(lax.broadcasted_iota(jnp.int32, (tq, tq), 1))
                s = jnp.where(cc <= rc, s, NEG_INF)
            m_old = m_ref[hh]
            m_new = jnp.maximum(m_old, jnp.max(s, axis=-1, keepdims=True))
            alpha = jnp.exp2((m_old - m_new) * MLA_EXP2_SCALE)
            p = jnp.exp2((s - jnp.tile(m_new, (1, tq // LANES))) * MLA_EXP2_SCALE)
            l_ref[hh] = alpha * l_ref[hh] + jnp.sum(p, axis=-1, keepdims=True)
            pv = _dot(p.astype(BF), v_ref[pl.ds(off, tq), hh * V_HEAD:(hh + 1) * V_HEAD])
            acc_ref[hh] = alpha * acc_ref[hh] + pv
            m_ref[hh] = m_new

    def body(ki, carry):
        step(ki, False)
        return carry

    lax.fori_loop(0, qi, body, 0)
    step(qi, True)
    for hh in range(MLA_HPS):
        o_ref[:, hh * V_HEAD:(hh + 1) * V_HEAD] = (acc_ref[hh] / l_ref[hh]).astype(BF)


def _mla_attn_prompt(qf, kf, v, nb, seq):
    tq = min(MLA_TQ, seq)
    nq = seq // tq
    hps = MLA_HPS
    return pl.pallas_call(
        functools.partial(_mla_prompt_body, tq=tq),
        grid=(nb, MLA_HEADS // hps, nq),
        in_specs=[pl.BlockSpec((tq, hps * HEAD_PAD), lambda b, h, i: (b * nq + i, h)),
                  pl.BlockSpec((seq, hps * HEAD_PAD), lambda b, h, i: (b, h)),
                  pl.BlockSpec((seq, hps * V_HEAD), lambda b, h, i: (b, h))],
        out_specs=pl.BlockSpec((tq, hps * V_HEAD), lambda b, h, i: (b * nq + i, h)),
        out_shape=jax.ShapeDtypeStruct((nb * seq, MLA_HEADS * V_HEAD), BF),
        scratch_shapes=[pltpu.VMEM((hps, tq, LANES), F32), pltpu.VMEM((hps, tq, LANES), F32),
                        pltpu.VMEM((hps, tq, V_HEAD), F32)],
        compiler_params=_cparams(("parallel", "parallel", "arbitrary")),
        name="mla_attn_prompt",
    )(qf, kf, v)


def _mla_sample_body(q_ref, k_ref, v_ref, o_ref, *, past, n_keys):
    nq = q_ref.shape[0]
    nk = k_ref.shape[0]
    s = _dot_t(q_ref[...], k_ref[...]) * MLA_SCALE
    qpos = past + lax.broadcasted_iota(jnp.int32, (nq, nk), 0)
    kpos = lax.broadcasted_iota(jnp.int32, (nq, nk), 1)
    s = jnp.where(kpos < n_keys, jnp.where(_chunk_of(kpos) <= _chunk_of(qpos), s, NEG_INF), NEG_INF)
    e = jnp.exp(s - jnp.max(s, axis=-1, keepdims=True))
    p = e / jnp.sum(e, axis=-1, keepdims=True)
    o_ref[...] = _dot(p.astype(BF), v_ref[...]).astype(BF)


def _mla_attn_sample(qf, kf, v, nb, nq, nk, past, n_keys):
    return pl.pallas_call(
        functools.partial(_mla_sample_body, past=past, n_keys=n_keys),
        grid=(nb, MLA_HEADS),
        in_specs=[pl.BlockSpec((nq, HEAD_PAD), lambda b, h: (b, h)),
                  pl.BlockSpec((nk, HEAD_PAD), lambda b, h: (b, h)),
                  pl.BlockSpec((nk, V_HEAD), lambda b, h: (b, h))],
        out_specs=pl.BlockSpec((nq, V_HEAD), lambda b, h: (b, h)),
        out_shape=jax.ShapeDtypeStruct((nb * nq, MLA_HEADS * V_HEAD), BF),
        compiler_params=_cparams(("parallel", "parallel")),
        name="mla_attn_sample",
    )(qf, kf, v)


BAND_TQ = 256
BAND_BLOCKS = LEFT_FRAMES // BAND_TQ + 1


def _band_prompt_body(q_ref, *refs):
    k_refs = refs[:BAND_BLOCKS]
    v_refs = refs[BAND_BLOCKS:2 * BAND_BLOCKS]
    bias_ref, o_ref = refs[2 * BAND_BLOCKS:]
    tq = q_ref.shape[0]
    for h in range(C_HEADS):
        sl = slice(h * C_HEAD_DIM, (h + 1) * C_HEAD_DIM)
        qh = q_ref[:, sl]
        s = jnp.concatenate([_dot_t(qh, kr[:, sl]) for kr in k_refs], axis=-1)
        s = s * C_SCALE + bias_ref[0, h]
        e = jnp.exp(s - jnp.max(s, axis=-1, keepdims=True))
        p = (e / jnp.sum(e, axis=-1, keepdims=True)).astype(BF)
        o = _dot(p[:, 0:tq], v_refs[0][:, sl])
        for j in range(1, BAND_BLOCKS):
            o = o + _dot(p[:, j * tq:(j + 1) * tq], v_refs[j][:, sl])
        o_ref[:, sl] = o.astype(BF)


def _band_attn_prompt(q, k, v, bias, nb, seq):
    tq = BAND_TQ
    nq = seq // tq
    hc = C_HEADS * C_HEAD_DIM
    back = BAND_BLOCKS - 1

    def kv_spec(j):
        return pl.BlockSpec((tq, hc), lambda b, i: (b * nq + jnp.maximum(i - (back - j), 0), 0))

    kv_specs = [kv_spec(j) for j in range(BAND_BLOCKS)]
    return pl.pallas_call(
        _band_prompt_body,
        grid=(nb, nq),
        in_specs=[pl.BlockSpec((tq, hc), lambda b, i: (b * nq + i, 0))] + kv_specs + kv_specs
        + [pl.BlockSpec((1, C_HEADS, tq, BAND_BLOCKS * tq), lambda b, i: (jnp.minimum(i, back), 0, 0, 0))],
        out_specs=pl.BlockSpec((tq, hc), lambda b, i: (b * nq + i, 0)),
        out_shape=jax.ShapeDtypeStruct((nb * seq, hc), BF),
        compiler_params=_cparams(("parallel", "arbitrary")),
        name="band_attn_prompt",
    )(q, *([k] * BAND_BLOCKS), *([v] * BAND_BLOCKS), bias)


def _band_sample_body(q_ref, k_ref, v_ref, bias_ref, o_ref):
    for h in range(C_HEADS):
        sl = slice(h * C_HEAD_DIM, (h + 1) * C_HEAD_DIM)
        s = _dot_t(q_ref[:, sl], k_ref[0, :, sl].astype(BF)) * C_SCALE + bias_ref[h]
        e = jnp.exp(s - jnp.max(s, axis=-1, keepdims=True))
        p = (e / jnp.sum(e, axis=-1, keepdims=True)).astype(BF)
        o_ref[:, sl] = _dot(p, v_ref[0, :, sl].astype(BF)).astype(BF)


def _band_attn_sample(q, k_all, v_all, bias, nb, nq):
    hc = C_HEADS * C_HEAD_DIM
    nk = k_all.shape[1]
    return pl.pallas_call(
        _band_sample_body,
        grid=(nb,),
        in_specs=[pl.BlockSpec((nq, hc), lambda b: (b, 0)),
                  pl.BlockSpec((1, nk, hc), lambda b: (b, 0, 0)),
                  pl.BlockSpec((1, nk, hc), lambda b: (b, 0, 0)),
                  pl.BlockSpec((C_HEADS, nq, nk), lambda b: (0, 0, 0))],
        out_specs=pl.BlockSpec((nq, hc), lambda b: (b, 0)),
        out_shape=jax.ShapeDtypeStruct((nb * nq, hc), BF),
        compiler_params=_cparams(("parallel",)),
        name="band_attn_sample",
    )(q, k_all, v_all, bias)


MERGE_NB = 256


def _merge_body(h_ref, hb_ref, ca_ref, ob_ref, oc_ref, wg0_ref, wg1_ref, wg2_ref, wpw_ref, wmo_ref, wco_ref,
                wo_ref, g_ref, b_ref, o_ref, acc_ref, *, nj):
    j = pl.program_id(1)

    @pl.when(j == 0)
    def _():
        acc_ref[...] = jnp.zeros_like(acc_ref)

    hb = hb_ref[...]
    merged = jax.nn.sigmoid(_dot(hb, wg0_ref[...])) * _dot(ca_ref[...], wpw_ref[...])
    merged = merged + jax.nn.sigmoid(_dot(hb, wg1_ref[...])) * _dot(ob_ref[...], wmo_ref[...])
    merged = merged + jax.nn.sigmoid(_dot(hb, wg2_ref[...])) * _dot(oc_ref[...], wco_ref[...])
    acc_ref[...] += _dot(merged.astype(BF), wo_ref[...])

    @pl.when(j == nj - 1)
    def _():
        o_ref[...] = _layer_norm(DN_ALPHA * h_ref[...] + acc_ref[...], g_ref[...], b_ref[...])


def _merge_ln(h, hb, ca, ob, oc, w_gate, wpw, wmo, wco, wo, g, b):
    t, d = h.shape
    tm = min(512, t)
    nbk = MERGE_NB
    nj = d // nbk
    tok = lambda width: pl.BlockSpec((tm, width), lambda i, j: (i, 0))
    gate = lambda br: pl.BlockSpec((d, nbk), lambda i, j: (0, br * nj + j))
    col = lambda rows: pl.BlockSpec((rows, nbk), lambda i, j: (0, j))
    vec = pl.BlockSpec((1, d), lambda i, j: (0, 0))
    return pl.pallas_call(
        functools.partial(_merge_body, nj=nj),
        grid=(t // tm, nj),
        in_specs=[tok(d), tok(d), tok(C_CONV), tok(MLA_HEADS * V_HEAD), tok(C_HEADS * C_HEAD_DIM),
                  gate(0), gate(1), gate(2),
                  col(C_CONV), col(MLA_HEADS * V_HEAD), col(C_HEADS * C_HEAD_DIM),
                  pl.BlockSpec((nbk, d), lambda i, j: (j, 0)), vec, vec],
        out_specs=pl.BlockSpec((tm, d), lambda i, j: (i, 0)),
        out_shape=jax.ShapeDtypeStruct((t, d), F32),
        scratch_shapes=[pltpu.VMEM((tm, d), F32)],
        compiler_params=_cparams(("parallel", "arbitrary")),
        name="merge_ln",
    )(h, hb, ca, ob, oc, w_gate, w_gate, w_gate, wpw, wmo, wco, wo, g.reshape(1, d), b.reshape(1, d))


def _ple_body(x_ref, p_ref, wg_ref, wp_ref, o_ref):
    x = x_ref[...]
    gate = jax.nn.sigmoid(_dot(x.astype(BF), wg_ref[...]))
    o_ref[...] = x + gate * _dot(p_ref[...].astype(BF), wp_ref[...])


def _ple(x, p, wg, wp):
    t, d = x.shape
    tm = min(512, t)
    return pl.pallas_call(
        _ple_body,
        grid=(t // tm,),
        in_specs=[pl.BlockSpec((tm, d), lambda i: (i, 0)),
                  pl.BlockSpec((tm, PLE_DIM), lambda i: (i, 0)),
                  pl.BlockSpec((d, d), lambda i: (0, 0)),
                  pl.BlockSpec((PLE_DIM, d), lambda i: (0, 0))],
        out_specs=pl.BlockSpec((tm, d), lambda i: (i, 0)),
        out_shape=jax.ShapeDtypeStruct((t, d), F32),
        compiler_params=_cparams(("parallel",)),
        name="ple",
    )(x, p, wg, wp)


def _rope_tables(pos):
    half = QK_ROPE // 2
    inv = ROPE_THETA ** (-jnp.arange(half, dtype=F32) / half)
    ang = pos.astype(F32)[:, None] * inv[None, :]
    cos, sin = jnp.cos(ang), jnp.sin(ang)
    z = jnp.zeros_like(cos)
    c = jnp.concatenate([cos, cos, z, z], axis=-1)
    sa = jnp.concatenate([-sin, z, z, z], axis=-1)
    sb = jnp.concatenate([z, sin, z, z], axis=-1)
    return c, sa, sb


def _band_bias_body(rb_ref, o_ref, *, q0s, k0s, n_valid):
    h = pl.program_id(0)
    nq, nk = o_ref.shape[2], o_ref.shape[3]
    rows = min(nq, 64)
    row0 = lax.broadcasted_iota(jnp.int32, (rows, LANES), 0)
    col0 = lax.broadcasted_iota(jnp.int32, (rows, LANES), 1)
    blocks = {}
    for r0 in range(0, nq, rows):
        for c0 in range(0, nk, LANES):
            row, col = row0 + r0, col0 + c0
            d = (q0s[0] + r0) - (k0s[0] + c0)
            clip = lambda x: min(max(x, -REL_MAX), REL_MAX) + REL_MAX
            lo, hi = clip(d - (LANES - 1)), clip(d + rows - 1)
            if d not in blocks:
                rel = jnp.clip(d + row0 - col0, -REL_MAX, REL_MAX) + REL_MAX

                def pick(r, acc, rel=rel):
                    return jnp.where(rel == r, rb_ref[h, r], acc)

                first = jnp.full((rows, LANES), rb_ref[h, lo], F32)
                blocks[d] = lax.fori_loop(lo + 1, hi + 1, pick, first)
            bias = blocks[d]
            for v, (q0, k0) in enumerate(zip(q0s, k0s)):
                kpos = k0 + col
                qc = _chunk_of(q0 + row)
                kc = _chunk_of(jnp.maximum(kpos, 0))
                vis = (kpos >= 0) & (kc <= qc) & (kc >= qc - LEFT_CHUNKS) & (col < n_valid)
                o_ref[v, 0, r0:r0 + rows, c0:c0 + LANES] = jnp.where(vis, bias, NEG_INF)


def _band_bias(rel_bias, q0s, k0s, nq, nk, n_valid):
    assert len({q0 - k0 for q0, k0 in zip(q0s, k0s)}) == 1
    nv = len(q0s)
    return pl.pallas_call(
        functools.partial(_band_bias_body, q0s=tuple(q0s), k0s=tuple(k0s), n_valid=n_valid),
        grid=(C_HEADS,),
        in_specs=[pl.BlockSpec(memory_space=pltpu.SMEM)],
        out_specs=pl.BlockSpec((nv, 1, nq, nk), lambda h: (0, h, 0, 0)),
        out_shape=jax.ShapeDtypeStruct((nv, C_HEADS, nq, nk), F32),
        compiler_params=_cparams(("parallel",)),
        name="band_bias",
    )(rel_bias)


def _layer_weights(i, a):
    bf = lambda w: w.astype(BF)
    w_in = a["w_in"][i]
    c0, c1, c2, c3 = IN_CONV, IN_CONV + IN_Q, IN_CONV + IN_Q + IN_KV, IN_CONV + IN_Q + IN_KV + IN_C
    q_up = a["w_q_up"][i].reshape(Q_LORA, MLA_HEADS, QK_NOPE + QK_ROPE)
    q_up = jnp.pad(q_up, ((0, 0), (0, 0), (0, HEAD_PAD - QK_NOPE - QK_ROPE))).reshape(Q_LORA, MLA_HEADS * HEAD_PAD)
    kv_up = a["w_kv_up"][i].reshape(KV_LORA, MLA_HEADS, QK_NOPE + V_HEAD)
    return dict(
        ln1_g=a["ln1_g"][i], ln1_b=a["ln1_b"][i], ln2_g=a["ln2_g"][i], ln2_b=a["ln2_b"][i],
        ln3_g=a["ln3_g"][i], ln3_b=a["ln3_b"][i],
        ffn1=(bf(a["w_ffn1_gate"][i]), bf(a["w_ffn1_up"][i]), bf(a["w_ffn1_down"][i])),
        ffn2=(bf(a["w_ffn2_gate"][i]), bf(a["w_ffn2_up"][i]), bf(a["w_ffn2_down"][i])),
        w_glu=bf(w_in[:, :c0]),
        w_cq=bf(w_in[:, c0:c1]),
        w_ckv=bf(jnp.pad(w_in[:, c1:c2], ((0, 0), (0, LANES - QK_ROPE)))),
        w_cqkv=bf(w_in[:, c2:c3]),
        w_gate=bf(w_in[:, c3:]),
        conv_w=a["conv_dw_w"][i], conv_b=a["conv_dw_b"][i], conv_g=a["conv_ln_g"][i], conv_beta=a["conv_ln_b"][i],
        w_pw=bf(a["w_conv_pw"][i]),
        q_norm=a["mla_q_norm"][i], q_up=bf(q_up), kv_norm=a["mla_kv_norm"][i],
        w_kn=bf(kv_up[:, :, :QK_NOPE].reshape(KV_LORA, MLA_HEADS * QK_NOPE)),
        w_v=bf(kv_up[:, :, QK_NOPE:].reshape(KV_LORA, MLA_HEADS * V_HEAD)),
        w_mo=bf(a["w_mla_o"][i]), rel_bias=a["rel_bias"][i], w_co=bf(a["w_chunk_o"][i]), w_o=bf(a["w_o"][i]),
        ple_gate=bf(a["w_ple_gate"][i]), ple_proj=bf(a["w_ple_proj"][i]),
    )


def _layer(x, p_l, w, nb, seq, tabs, bias, cache):
    t = nb * seq
    tm = min(512, t)
    h, hb = _ffn_ln(x, *w["ffn1"], w["ln1_g"], w["ln1_b"], emit_bf=True)

    u = _glu_proj(hb, w["w_glu"])
    if cache is None:
        hist = jnp.zeros((nb, HIST_PAD, C_CONV), F32)
    else:
        hist = jnp.pad(cache["conv"], ((0, 0), (HIST_PAD - (CONV_WIDTH - 1), 0), (0, 0)))
    ca = _conv_branch(u, hist, w["conv_w"], w["conv_b"], w["conv_g"], w["conv_beta"], nb, seq)
    u3 = u.reshape(nb, seq, C_CONV)
    if seq >= CONV_WIDTH - 1:
        conv_state = u3[:, seq - (CONV_WIDTH - 1):]
    else:
        conv_state = jnp.concatenate([cache["conv"], u3], axis=1)[:, -(CONV_WIDTH - 1):]

    qf = _mla_q_proj(hb, w["w_cq"], w["q_norm"], w["q_up"], tabs, tm)
    ckv, k_rope, krp = _mla_kv_latent(hb, w["w_ckv"], w["kv_norm"], tabs, tm)
    if cache is None:
        kf, v = _mla_kv_up(ckv, krp, w["w_kn"], w["w_v"], tm)
        o_b = _mla_attn_prompt(qf, kf, v, nb, seq)
    else:
        past = cache["ckv"].shape[1]
        n_keys = past + seq
        nk = -(-n_keys // LANES) * LANES
        tail = ((0, 0), (0, nk - n_keys), (0, 0))
        ckv_all = jnp.pad(jnp.concatenate([cache["ckv"], ckv.reshape(nb, seq, KV_LORA)], axis=1), tail)
        krope_c = jnp.pad(cache["krope"], ((0, 0), (0, 0), (0, LANES - QK_ROPE)))
        krp_all = jnp.pad(jnp.concatenate([krope_c, krp.reshape(nb, seq, LANES)], axis=1), tail)
        kf, v = _mla_kv_up(ckv_all.reshape(nb * nk, KV_LORA), krp_all.reshape(nb * nk, LANES),
                           w["w_kn"], w["w_v"], nk)
        o_b = _mla_attn_sample(qf, kf, v, nb, seq, nk, past, n_keys)

    hc = C_HEADS * C_HEAD_DIM
    if cache is None:
        keep = min(LEFT_FRAMES, seq)
        qc, kc, vc, ks, vs = _chunk_qkv_proj(hb, w["w_cqkv"], seq, keep, min(tm, keep))
        o_c = _band_attn_prompt(qc, kc, vc, bias, nb, seq)
        ck_state = ks.reshape(nb, keep, C_HEADS, C_HEAD_DIM)
        cv_state = vs.reshape(nb, keep, C_HEADS, C_HEAD_DIM)
    else:
        keep = cache["ck"].shape[1]
        qc, _, _, ks, vs = _chunk_qkv_proj(hb, w["w_cqkv"], seq, seq, seq)
        n_keys = keep + seq
        nk = -(-n_keys // LANES) * LANES
        tail = ((0, 0), (0, nk - n_keys), (0, 0))
        k_all = jnp.concatenate([cache["ck"].reshape(nb, keep, hc), ks.reshape(nb, seq, hc)], axis=1)
        v_all = jnp.concatenate([cache["cv"].reshape(nb, keep, hc), vs.reshape(nb, seq, hc)], axis=1)
        o_c = _band_attn_sample(qc, jnp.pad(k_all, tail), jnp.pad(v_all, tail), bias, nb, seq)
        ck_state = k_all[:, -keep:].reshape(nb, keep, C_HEADS, C_HEAD_DIM)
        cv_state = v_all[:, -keep:].reshape(nb, keep, C_HEADS, C_HEAD_DIM)

    x = _merge_ln(h, hb, ca, o_b, o_c, w["w_gate"], w["w_pw"], w["w_mo"], w["w_co"], w["w_o"],
                  w["ln2_g"], w["ln2_b"])
    x = _ffn_ln(x, *w["ffn2"], w["ln3_g"], w["ln3_b"], emit_bf=False)
    x = _ple(x, p_l, w["ple_gate"], w["ple_proj"])
    state = (ckv.reshape(nb, seq, KV_LORA), k_rope.reshape(nb, seq, QK_ROPE), ck_state, cv_state, conv_state)
    return x, state


def kernel(x_prompt, x_sample, cache_mla_ckv, cache_mla_krope, cache_chunk_k, cache_chunk_v, state_conv, p_prompt, p_sample, ln1_g, ln1_b, w_ffn1_gate, w_ffn1_up, w_ffn1_down, w_in, conv_dw_w, conv_dw_b, conv_ln_g, conv_ln_b, w_conv_pw, mla_q_norm, w_q_up, mla_kv_norm, w_kv_up, w_mla_o, rel_bias, w_chunk_o, w_o, ln2_g, ln2_b, w_ffn2_gate, w_ffn2_up, w_ffn2_down, ln3_g, ln3_b, w_ple_proj, w_ple_gate):
    a = dict(ln1_g=ln1_g, ln1_b=ln1_b, w_ffn1_gate=w_ffn1_gate, w_ffn1_up=w_ffn1_up, w_ffn1_down=w_ffn1_down,
             w_in=w_in, conv_dw_w=conv_dw_w, conv_dw_b=conv_dw_b, conv_ln_g=conv_ln_g, conv_ln_b=conv_ln_b,
             w_conv_pw=w_conv_pw, mla_q_norm=mla_q_norm, w_q_up=w_q_up, mla_kv_norm=mla_kv_norm,
             w_kv_up=w_kv_up, w_mla_o=w_mla_o, rel_bias=rel_bias, w_chunk_o=w_chunk_o, w_o=w_o,
             ln2_g=ln2_g, ln2_b=ln2_b, w_ffn2_gate=w_ffn2_gate, w_ffn2_up=w_ffn2_up, w_ffn2_down=w_ffn2_down,
             ln3_g=ln3_g, ln3_b=ln3_b, w_ple_proj=w_ple_proj, w_ple_gate=w_ple_gate)
    depth = w_in.shape[0]
    bp, sp, d = x_prompt.shape
    bs, ss, _ = x_sample.shape
    past = cache_mla_ckv.shape[2]
    keep_s = cache_chunk_k.shape[2]
    assert sp % BAND_TQ == 0 and sp % MLA_TQ == 0 and LEFT_FRAMES % BAND_TQ == 0

    pos_p = jnp.arange(sp, dtype=jnp.int32)
    pos_s = past + jnp.arange(ss, dtype=jnp.int32)
    tabs_p = _rope_tables(pos_p)
    tabs_s = tuple(jnp.tile(tb, (bs, 1)) for tb in _rope_tables(pos_s))

    nk_s = -(-(keep_s + ss) // LANES) * LANES

    xp = x_prompt.reshape(bp * sp, d)
    xs = x_sample.reshape(bs * ss, d)
    sp_all, ss_all = [], []
    for i in range(depth):
        w = _layer_weights(i, a)
        bias_p = _band_bias(w["rel_bias"], [v * BAND_TQ for v in range(BAND_BLOCKS)],
                            [v * BAND_TQ - LEFT_FRAMES for v in range(BAND_BLOCKS)],
                            BAND_TQ, BAND_BLOCKS * BAND_TQ, BAND_BLOCKS * BAND_TQ)
        bias_s = _band_bias(w["rel_bias"], [past], [past - keep_s], ss, nk_s, keep_s + ss)[0]
        xp, st_p = _layer(xp, p_prompt[i].reshape(bp * sp, PLE_DIM), w, bp, sp, tabs_p, bias_p, None)
        cache = dict(conv=state_conv[i], ckv=cache_mla_ckv[i], krope=cache_mla_krope[i],
                     ck=cache_chunk_k[i], cv=cache_chunk_v[i])
        xs, st_s = _layer(xs, p_sample[i].reshape(bs * ss, PLE_DIM), w, bs, ss, tabs_s, bias_s, cache)
        sp_all.append(st_p)
        ss_all.append(st_s)
    outs = [xp.reshape(bp, sp, d), xs.reshape(bs, ss, d)]
    for states in (sp_all, ss_all):
        for k in range(5):
            outs.append(jnp.stack([s[k] for s in states]))
    return tuple(outs)
```
